```python
import math
import jax, jax.numpy as jnp
from jax import lax
import numpy as np

D_MODEL = 2048
BATCH = 4
SEQ = 2048
DEPTH = 2

N_META = 16
N_A_LAYERS = DEPTH // 2
N_B_LAYERS = DEPTH - N_A_LAYERS
N_DENSE_LAYERS = (DEPTH + 1) // 2
N_MOE_LAYERS = DEPTH // 2
D_RNN = D_MODEL
RNN_BLOCK = 128
N_RNN_BLOCKS = D_RNN // RNN_BLOCK
CONV_WIDTH = 4
RGLRU_C = 8.0
HEAD_DIM = 128
N_HEADS = D_MODEL // (2 * HEAD_DIM)
V_HEAD_DIM = 2 * HEAD_DIM
Q_BLOCK = 128
D_FF = ((8 * D_MODEL // 3 + 127) // 128) * 128
N_EXPERTS = 8
TOP_K = 2
EPS = 1e-6

kernel_name = "yoco_rglru_diffattn_moe_trunk"


def rms_norm(x, g):
    xf = x.astype(jnp.float32)
    y = xf * lax.rsqrt(jnp.mean(xf * xf, axis=-1, keepdims=True) + EPS)
    return (y * g.astype(jnp.float32)).astype(x.dtype)


def swiglu(u, w_gate, w_up, w_down):
    return (jax.nn.silu(u @ w_gate) * (u @ w_up)) @ w_down


def alibi_slopes():
    return jnp.asarray(2.0 ** (-8.0 * np.arange(1, N_HEADS + 1) / N_HEADS), dtype=jnp.float32)


def causal_depthwise_conv(x, w, b):
    seq = x.shape[1]
    xp = jnp.pad(x, ((0, 0), (CONV_WIDTH - 1, 0), (0, 0)))
    y = b
    for k in range(CONV_WIDTH):
        y = y + xp[:, k:k + seq] * w[k]
    return y


def rglru_block(h, norm_g, w_in, conv_w, conv_b, w_rg, b_rg, w_ig, b_ig, lam, w_out):
    bsz, seq, _ = h.shape
    u = rms_norm(h, norm_g)
    x_br, g_br = jnp.split(u @ w_in, 2, axis=-1)
    xc = causal_depthwise_conv(x_br, conv_w, conv_b)
    xg = xc.reshape(bsz, seq, N_RNN_BLOCKS, RNN_BLOCK)
    r = jax.nn.sigmoid(jnp.einsum("blgi,gij->blgj", xg, w_rg).reshape(bsz, seq, D_RNN) + b_rg).astype(jnp.float32)
    i = jax.nn.sigmoid(jnp.einsum("blgi,gij->blgj", xg, w_ig).reshape(bsz, seq, D_RNN) + b_ig).astype(jnp.float32)
    log_a = -RGLRU_C * r * jax.nn.softplus(-lam.astype(jnp.float32))
    a = jnp.exp(log_a)
    b = jnp.sqrt(-jnp.expm1(2.0 * log_a)) * (i * xc.astype(jnp.float32))

    def combine(left, right):
        a_l, b_l = left
        a_r, b_r = right
        return a_l * a_r, a_r * b_l + b_r

    _, hs = lax.associative_scan(combine, (a, b), axis=1)
    y = hs.astype(h.dtype) * jax.nn.gelu(g_br)
    return y @ w_out


def diff_attention_block(h, norm_g, w_q, lq1, lk1, lq2, lk2, subln_g, w_out, k1, k2, v, lambda_init):
    bsz, seq, _ = h.shape
    n_blocks = seq // Q_BLOCK
    u = rms_norm(h, norm_g)
    q = (u @ w_q).reshape(bsz, seq, 2, N_HEADS, HEAD_DIM) * (HEAD_DIM ** -0.5)

    def to_blocks(t):
        return t.reshape(bsz, n_blocks, Q_BLOCK, N_HEADS, HEAD_DIM).transpose(1, 0, 3, 2, 4)

    q1b = to_blocks(q[:, :, 0])
    q2b = to_blocks(q[:, :, 1])
    q_pos = (N_META + jnp.arange(seq, dtype=jnp.int32)).reshape(n_blocks, Q_BLOCK)
    k_pos = jnp.arange(k1.shape[2], dtype=jnp.int32)
    slopes = alibi_slopes()
    lam = (jnp.exp(jnp.sum(lq1.astype(jnp.float32) * lk1.astype(jnp.float32)))
           - jnp.exp(jnp.sum(lq2.astype(jnp.float32) * lk2.astype(jnp.float32))) + lambda_init)
    v32 = v.astype(jnp.float32)

    def attend(args):
        q1_blk, q2_blk, qp = args
        dist = qp[:, None] - k_pos[None, :]
        bias = -slopes[:, None, None] * dist.astype(jnp.float32)
        visible = dist >= 0

        def probs(qb, kb):
            s = jnp.einsum("bhqd,bhkd->bhqk", qb, kb).astype(jnp.float32) + bias
            return jax.nn.softmax(jnp.where(visible, s, -jnp.inf), axis=-1)

        p = probs(q1_blk, k1) - lam * probs(q2_blk, k2)
        return jnp.einsum("bhqk,bhkd->bhqd", p, v32)

    o = lax.map(attend, (q1b, q2b, q_pos))
    o = o.transpose(1, 0, 3, 2, 4).reshape(bsz, seq, N_HEADS, V_HEAD_DIM)
    o = rms_norm(o, subln_g) * (1.0 - lambda_init)
    return o.reshape(bsz, seq, N_HEADS * V_HEAD_DIM).astype(h.dtype) @ w_out


def moe_swiglu(h, norm_g, router, w_gate, w_up, w_down):
    bsz, seq, d = h.shape
    u = rms_norm(h, norm_g).reshape(bsz * seq, d)
    logits = (u @ router).astype(jnp.float32)
    top_vals, top_idx = lax.top_k(logits, TOP_K)
    top_w = jax.nn.softmax(top_vals, axis=-1)
    gates = jnp.sum(jax.nn.one_hot(top_idx, N_EXPERTS, dtype=jnp.float32) * top_w[..., None], axis=1).astype(u.dtype)
    y = jnp.zeros_like(u)
    for e in range(N_EXPERTS):
        y = y + gates[:, e:e + 1] * swiglu(u, w_gate[e], w_up[e], w_down[e])
    return y.reshape(bsz, seq, d)


def setup_inputs(seed: int = 0) -> dict:
    key = jax.random.key(seed)
    ks = iter(jax.random.split(key, 40))

    def normal(shape, scale):
        return jax.random.normal(next(ks), shape, jnp.float32) * scale

    def gain(shape):
        return 1.0 + normal(shape, 0.02)

    d, f = D_MODEL, D_FF
    qk_width = 2 * N_HEADS * HEAD_DIM
    v_width = N_HEADS * V_HEAD_DIM
    x = normal((BATCH, SEQ, d), 1.0)
    meta_tokens = normal((N_META, d), 1.0)
    a_norm = gain((N_A_LAYERS, d))
    a_w_in = normal((N_A_LAYERS, d, 2 * D_RNN), d ** -0.5)
    a_conv_w = normal((N_A_LAYERS, CONV_WIDTH, D_RNN), CONV_WIDTH ** -0.5)
    a_conv_b = normal((N_A_LAYERS, D_RNN), 0.01)
    a_w_rgate = normal((N_A_LAYERS, N_RNN_BLOCKS, RNN_BLOCK, RNN_BLOCK), RNN_BLOCK ** -0.5)
    a_b_rgate = normal((N_A_LAYERS, D_RNN), 0.01)
    a_w_igate = normal((N_A_LAYERS, N_RNN_BLOCKS, RNN_BLOCK, RNN_BLOCK), RNN_BLOCK ** -0.5)
    a_b_igate = normal((N_A_LAYERS, D_RNN), 0.01)
    u = jax.random.uniform(next(ks), (N_A_LAYERS, D_RNN), jnp.float32, 0.9, 0.999)
    a0 = u ** (1.0 / RGLRU_C)
    a_lambda = jnp.log(a0) - jnp.log1p(-a0)
    a_w_out = normal((N_A_LAYERS, D_RNN, d), D_RNN ** -0.5)
    kv_norm = gain((d,))
    w_kv = normal((d, qk_width + v_width), d ** -0.5)
    b_norm = gain((N_B_LAYERS, d))
    b_w_q = normal((N_B_LAYERS, d, qk_width), d ** -0.5)
    b_lambda_q1 = normal((N_B_LAYERS, HEAD_DIM), 0.1)
    b_lambda_k1 = normal((N_B_LAYERS, HEAD_DIM), 0.1)
    b_lambda_q2 = normal((N_B_LAYERS, HEAD_DIM), 0.1)
    b_lambda_k2 = normal((N_B_LAYERS, HEAD_DIM), 0.1)
    b_subln = gain((N_B_LAYERS, V_HEAD_DIM))
    b_w_out = normal((N_B_LAYERS, v_width, d), v_width ** -0.5)
    ffn_norm = gain((DEPTH, d))
    ffn_w_gate = normal((N_DENSE_LAYERS, d, f), d ** -0.5)
    ffn_w_up = normal((N_DENSE_LAYERS, d, f), d ** -0.5)
    ffn_w_down = normal((N_DENSE_LAYERS, f, d), f ** -0.5)
    moe_router = normal((N_MOE_LAYERS, d, N_EXPERTS), d ** -0.5)
    moe_w_gate = normal((N_MOE_LAYERS, N_EXPERTS, d, f), d ** -0.5)
    moe_w_up = normal((N_MOE_LAYERS, N_EXPERTS, d, f), d ** -0.5)
    moe_w_down = normal((N_MOE_LAYERS, N_EXPERTS, f, d), f ** -0.5)
    final_norm = gain((d,))
    return {
        "x": x, "meta_tokens": meta_tokens,
        "a_norm": a_norm, "a_w_in": a_w_in, "a_conv_w": a_conv_w, "a_conv_b": a_conv_b,
        "a_w_rgate": a_w_rgate, "a_b_rgate": a_b_rgate, "a_w_igate": a_w_igate, "a_b_igate": a_b_igate,
        "a_lambda": a_lambda, "a_w_out": a_w_out,
        "kv_norm": kv_norm, "w_kv": w_kv,
        "b_norm": b_norm, "b_w_q": b_w_q, "b_lambda_q1": b_lambda_q1, "b_lambda_k1": b_lambda_k1,
        "b_lambda_q2": b_lambda_q2, "b_lambda_k2": b_lambda_k2, "b_subln": b_subln, "b_w_out": b_w_out,
        "ffn_norm": ffn_norm, "ffn_w_gate": ffn_w_gate, "ffn_w_up": ffn_w_up, "ffn_w_down": ffn_w_down,
        "moe_router": moe_router, "moe_w_gate": moe_w_gate, "moe_w_up": moe_w_up, "moe_w_down": moe_w_down,
        "final_norm": final_norm,
    }


def reference(x, meta_tokens, a_norm, a_w_in, a_conv_w, a_conv_b, a_w_rgate, a_b_rgate, a_w_igate, a_b_igate,
              a_lambda, a_w_out, kv_norm, w_kv, b_norm, b_w_q, b_lambda_q1, b_lambda_k1, b_lambda_q2, b_lambda_k2,
              b_subln, b_w_out, ffn_norm, ffn_w_gate, ffn_w_up, ffn_w_down, moe_router, moe_w_gate, moe_w_up,
              moe_w_down, final_norm):
    bsz = x.shape[0]
    h = jnp.concatenate([jnp.broadcast_to(meta_tokens.astype(x.dtype)[None], (bsz, N_META, D_MODEL)), x], axis=1)
    qk_width = 2 * N_HEADS * HEAD_DIM
    for layer in range(DEPTH):
        if layer < N_A_LAYERS:
            j = layer
            h = h + rglru_block(h, a_norm[j], a_w_in[j], a_conv_w[j], a_conv_b[j], a_w_rgate[j], a_b_rgate[j],
                                a_w_igate[j], a_b_igate[j], a_lambda[j], a_w_out[j])
        else:
            j = layer - N_A_LAYERS
            lambda_init = 0.8 - 0.6 * math.exp(-0.3 * layer)
            h = h + diff_attention_block(h, b_norm[j], b_w_q[j], b_lambda_q1[j], b_lambda_k1[j], b_lambda_q2[j],
                                         b_lambda_k2[j], b_subln[j], b_w_out[j], k1, k2, v, lambda_init)
        if layer % 2 == 0:
            m = layer // 2
            h = h + swiglu(rms_norm(h, ffn_norm[layer]), ffn_w_gate[m], ffn_w_up[m], ffn_w_down[m])
        else:
            m = layer // 2
            h = h + moe_swiglu(h, ffn_norm[layer], moe_router[m], moe_w_gate[m], moe_w_up[m], moe_w_down[m])
        if layer == N_A_LAYERS - 1:
            kv = rms_norm(h, kv_norm) @ w_kv
            seq_full = kv.shape[1]
            k = kv[..., :qk_width].reshape(bsz, seq_full, 2, N_HEADS, HEAD_DIM)
            k1 = k[:, :, 0].transpose(0, 2, 1, 3)
            k2 = k[:, :, 1].transpose(0, 2, 1, 3)
            v = kv[..., qk_width:].reshape(bsz, seq_full, N_HEADS, V_HEAD_DIM).transpose(0, 2, 1, 3)
            h = h[:, N_META:]
    return rms_norm(h, final_norm)
```

```python
import functools
import math

import jax
import jax.numpy as jnp
from jax import lax
from jax.experimental import pallas as pl
from jax.experimental.pallas import tpu as pltpu

F32 = jnp.float32
BF16 = jnp.bfloat16

D_MODEL = 2048
N_META = 16
CONV_WIDTH = 4
RGLRU_C = 8.0
RNN_BLOCK = 128
HEAD_DIM = 128
N_HEADS = 8
V_HEAD_DIM = 2 * HEAD_DIM
D_FF = 5504
N_EXPERTS = 8
TOP_K = 2
EPS = 1e-6

LANES = 128
FF_TAIL = LANES
FF_MAIN = D_FF - FF_TAIL
VMEM_LIMIT = 56 * 1024 * 1024


def _cparams(*sem):
    return pltpu.CompilerParams(dimension_semantics=sem, vmem_limit_bytes=VMEM_LIMIT)


def _rms(x, g):
    return x * lax.rsqrt(jnp.mean(x * x, axis=-1, keepdims=True) + EPS) * g


def _sigmoid(x):
    return 1.0 / (1.0 + jnp.exp(-x))


def _silu(x):
    return x * _sigmoid(x)


def _gelu_tanh(x):
    return 0.5 * x * (1.0 + jnp.tanh(math.sqrt(2.0 / math.pi) * (x + 0.044715 * (x * x * x))))


def _bdot(a, b):
    return jnp.dot(a, b, preferred_element_type=F32)


def _norm_matmul_kernel(x_ref, g_ref, w_ref, o_ref, xn_ref, *, scale):
    @pl.when(pl.program_id(1) == 0)
    def _():
        xn_ref[...] = _rms(x_ref[...], g_ref[...]).astype(BF16)

    acc = _bdot(xn_ref[...], w_ref[...].astype(BF16))
    if scale != 1.0:
        acc = acc * scale
    o_ref[...] = acc.astype(o_ref.dtype)


def _norm_matmul(x, g, w, *, tm, tn, out_dtype, scale=1.0):
    m, k = x.shape
    n = w.shape[1]
    return pl.pallas_call(
        functools.partial(_norm_matmul_kernel, scale=scale),
        grid=(m // tm, n // tn),
        in_specs=[
            pl.BlockSpec((tm, k), lambda i, j: (i, 0)),
            pl.BlockSpec((1, k), lambda i, j: (0, 0)),
            pl.BlockSpec((k, tn), lambda i, j: (0, j)),
        ],
        out_specs=pl.BlockSpec((tm, tn), lambda i, j: (i, j)),
        out_shape=jax.ShapeDtypeStruct((m, n), out_dtype),
        scratch_shapes=[pltpu.VMEM((tm, k), BF16)],
        compiler_params=_cparams("parallel", "arbitrary"),
        name="norm_matmul",
    )(x, g.reshape(1, k), w)


def _matmul_res_kernel(y_ref, w_ref, r_ref, o_ref):
    o_ref[...] = r_ref[...] + _bdot(y_ref[...], w_ref[...].astype(BF16))


def _matmul_res(y, w, res, *, tm, tn):
    m, k = y.shape
    n = w.shape[1]
    return pl.pallas_call(
        _matmul_res_kernel,
        grid=(m // tm, n // tn),
        in_specs=[
            pl.BlockSpec((tm, k), lambda i, j: (i, 0)),
            pl.BlockSpec((k, tn), lambda i, j: (0, j)),
            pl.BlockSpec((tm, tn), lambda i, j: (i, j)),
        ],
        out_specs=pl.BlockSpec((tm, tn), lambda i, j: (i, j)),
        out_shape=jax.ShapeDtypeStruct((m, n), F32),
        compiler_params=_cparams("parallel", "arbitrary"),
        name="matmul_res",
    )(y, w, res)


def _rglru_kernel(x_ref, g_ref, cw_ref, cb_ref, wr_ref, br_ref, wi_ref, bi_ref, lam_ref, y_ref, a_s, b_s):
    seq, c = a_s.shape
    x = x_ref[0]
    row = lax.broadcasted_iota(jnp.int32, (seq, c), 0)
    xc = jnp.broadcast_to(cb_ref[...], (seq, c))
    for k in range(CONV_WIDTH):
        shift = CONV_WIDTH - 1 - k
        xs = x if shift == 0 else jnp.where(row >= shift, pltpu.roll(x, shift, 0), 0.0)
        xc = xc + xs * cw_ref[k:k + 1, :]

    lam = lam_ref[...]
    softplus_neg_lam = jnp.maximum(-lam, 0.0) + jnp.log1p(jnp.exp(-jnp.abs(lam)))
    for gi in range(c // RNN_BLOCK):
        sl = slice(gi * RNN_BLOCK, (gi + 1) * RNN_BLOCK)
        xcg = xc[:, sl]
        xb = xcg.astype(BF16)
        r = _sigmoid(_bdot(xb, wr_ref[gi].astype(BF16)) + br_ref[:, sl])
        ig = _sigmoid(_bdot(xb, wi_ref[gi].astype(BF16)) + bi_ref[:, sl])
        log_a = (-RGLRU_C) * r * softplus_neg_lam[:, sl]
        a = jnp.exp(log_a)
        a_s[:, sl] = a
        b_s[:, sl] = jnp.sqrt(jnp.tanh(-log_a) * (1.0 + a * a)) * (ig * xcg)

    sub = lax.broadcasted_iota(jnp.int32, (8, c), 0)

    def tile(i, h_prev):
        r0 = pl.multiple_of(i * 8, 8)
        a = a_s[pl.ds(r0, 8), :]
        b = b_s[pl.ds(r0, 8), :]
        for s in (1, 2, 4):
            keep = sub >= s
            a_sh = jnp.where(keep, pltpu.roll(a, s, 0), 1.0)
            b_sh = jnp.where(keep, pltpu.roll(b, s, 0), 0.0)
            b = a * b_sh + b
            a = a * a_sh
        h = a * h_prev + b
        b_s[pl.ds(r0, 8), :] = h
        return jnp.broadcast_to(h[7:8, :], (8, c))

    lax.fori_loop(0, seq // 8, tile, jnp.zeros((8, c), F32))
    y_ref[0] = (b_s[...] * _gelu_tanh(g_ref[0])).astype(BF16)


def _rglru(xg, conv_w, conv_b, w_rg, b_rg, w_ig, b_ig, lam, *, c):
    bsz, seq, two_d = xg.shape
    d = two_d // 2
    nc = d // c
    gpb = c // RNN_BLOCK
    vec = lambda v: v.reshape(1, d)
    vspec = pl.BlockSpec((1, c), lambda b, j: (0, j))
    return pl.pallas_call(
        _rglru_kernel,
        grid=(bsz, nc),
        in_specs=[
            pl.BlockSpec((1, seq, c), lambda b, j: (b, 0, j)),
            pl.BlockSpec((1, seq, c), lambda b, j: (b, 0, nc + j)),
            pl.BlockSpec((CONV_WIDTH, c), lambda b, j: (0, j)),
            vspec,
            pl.BlockSpec((gpb, RNN_BLOCK, RNN_BLOCK), lambda b, j: (j, 0, 0)),
            vspec,
            pl.BlockSpec((gpb, RNN_BLOCK, RNN_BLOCK), lambda b, j: (j, 0, 0)),
            vspec,
            vspec,
        ],
        out_specs=pl.BlockSpec((1, seq, c), lambda b, j: (b, 0, j)),
        out_shape=jax.ShapeDtypeStruct((bsz, seq, d), BF16),
        scratch_shapes=[pltpu.VMEM((seq, c), F32), pltpu.VMEM((seq, c), F32)],
        compiler_params=_cparams("parallel", "parallel"),
        name="rglru",
    )(xg, xg, conv_w, vec(conv_b), w_rg, vec(b_rg), w_ig, vec(b_ig), vec(lam))


def _ffn_kernel(x_ref, g_ref, wg_ref, wu_ref, wd_ref, wgt_ref, wut_ref, wdt_ref, o_ref, xn_ref):
    j = pl.program_id(1)

    def contrib(wg, wu, wd):
        xn = xn_ref[...]
        h = _silu(_bdot(xn, wg[...].astype(BF16))) * _bdot(xn, wu[...].astype(BF16))
        return _bdot(h.astype(BF16), wd[...].astype(BF16))

    @pl.when(j == 0)
    def _():
        x = x_ref[...]
        xn_ref[...] = _rms(x, g_ref[...]).astype(BF16)
        o_ref[...] = x + contrib(wgt_ref, wut_ref, wdt_ref)

    o_ref[...] += contrib(wg_ref, wu_ref, wd_ref)


def _ffn(x, g, w_gate, w_up, w_down, *, tm, tf):
    m, d = x.shape
    tail_blk = FF_MAIN // FF_TAIL
    return pl.pallas_call(
        _ffn_kernel,
        grid=(m // tm, FF_MAIN // tf),
        in_specs=[
            pl.BlockSpec((tm, d), lambda i, j: (i, 0)),
            pl.BlockSpec((1, d), lambda i, j: (0, 0)),
            pl.BlockSpec((d, tf), lambda i, j: (0, j)),
            pl.BlockSpec((d, tf), lambda i, j: (0, j)),
            pl.BlockSpec((tf, d), lambda i, j: (j, 0)),
            pl.BlockSpec((d, FF_TAIL), lambda i, j: (0, tail_blk)),
            pl.BlockSpec((d, FF_TAIL), lambda i, j: (0, tail_blk)),
            pl.BlockSpec((FF_TAIL, d), lambda i, j: (tail_blk, 0)),
        ],
        out_specs=pl.BlockSpec((tm, d), lambda i, j: (i, 0)),
        out_shape=jax.ShapeDtypeStruct((m, d), F32),
        scratch_shapes=[pltpu.VMEM((tm, d), BF16)],
        compiler_params=_cparams("parallel", "arbitrary"),
        name="dense_ffn",
    )(x, g.reshape(1, d), w_gate, w_up, w_down, w_gate, w_up, w_down)


def _attn_kernel(slopes_ref, q1_ref, q2_ref, k1_ref, k2_ref, v_ref, lq1_ref, lk1_ref, lq2_ref, lk2_ref,
                 sg_ref, o_ref, m_s, l_s, acc_s, *, tq, lambda_init):
    head = pl.program_id(1)
    qi = pl.program_id(2)
    slope = slopes_ref[head]
    col = lax.broadcasted_iota(jnp.int32, (1, tq), 1)
    qs = (q1_ref[0], q2_ref[0])
    ks = (k1_ref, k2_ref)
    nt = (((1,), (1,)), ((), ()))

    def step(s, kc, vc, bias, mask, init):
        sc = lax.dot_general(qs[s], kc, nt, preferred_element_type=F32) + bias
        if mask is not None:
            sc = jnp.where(mask, sc, -jnp.inf)
        mx = jnp.max(sc, axis=-1, keepdims=True)
        if init:
            p = jnp.exp(sc - mx)
            m_s[s] = mx
            l_s[s] = jnp.sum(p, axis=-1, keepdims=True)
            acc_s[s] = _bdot(p.astype(BF16), vc)
        else:
            m_old = m_s[s]
            m_new = jnp.maximum(m_old, mx)
            alpha = jnp.exp(m_old - m_new)
            p = jnp.exp(sc - m_new)
            m_s[s] = m_new
            l_s[s] = alpha * l_s[s] + jnp.sum(p, axis=-1, keepdims=True)
            acc_s[s] = alpha * acc_s[s] + _bdot(p.astype(BF16), vc)

    d0 = pl.multiple_of(N_META + qi * tq, 16)
    diag_mask = col <= lax.broadcasted_iota(jnp.int32, (tq, tq), 0)
    diag_bias = slope * col.astype(F32)
    vd = v_ref[0, pl.ds(d0, tq), :]
    for s in range(2):
        step(s, ks[s][0, pl.ds(d0, tq), :], vd, diag_bias, diag_mask, True)

    lane = lax.broadcasted_iota(jnp.int32, (1, LANES), 1)
    meta_mask = lane < N_META
    meta_bias = slope * (lane - N_META - qi * tq).astype(F32)
    vm = v_ref[0, 0:LANES, :]
    for s in range(2):
        step(s, ks[s][0, 0:LANES, :], vm, meta_bias, meta_mask, False)

    def body(c, carry):
        c0 = pl.multiple_of(N_META + c * tq, 16)
        bias = slope * ((c - qi) * tq + col).astype(F32)
        vc = v_ref[0, pl.ds(c0, tq), :]
        for s in range(2):
            step(s, ks[s][0, pl.ds(c0, tq), :], vc, bias, None, False)
        return carry

    lax.fori_loop(0, qi, body, 0)

    lam = (jnp.exp(jnp.sum(lq1_ref[...] * lk1_ref[...], axis=-1, keepdims=True))
           - jnp.exp(jnp.sum(lq2_ref[...] * lk2_ref[...], axis=-1, keepdims=True)) + lambda_init)
    o = acc_s[0] / l_s[0] - lam * (acc_s[1] / l_s[1])
    o_ref[0] = (_rms(o, sg_ref[...]) * (1.0 - lambda_init)).astype(o_ref.dtype)


def _diff_attention(q, kv, lq1, lk1, lq2, lk2, subln_g, *, tq, lambda_init):
    bsz, seq, _ = q.shape
    full = kv.shape[1]
    slopes = jnp.asarray([2.0 ** (-8.0 * (i + 1) / N_HEADS) for i in range(N_HEADS)], F32)
    vec = lambda v: v.reshape(1, -1)
    lspec = pl.BlockSpec((1, HEAD_DIM), lambda b, h, i, s: (0, 0))
    grid_spec = pltpu.PrefetchScalarGridSpec(
        num_scalar_prefetch=1,
        grid=(bsz, N_HEADS, seq // tq),
        in_specs=[
            pl.BlockSpec((1, tq, HEAD_DIM), lambda b, h, i, s: (b, i, h)),
            pl.BlockSpec((1, tq, HEAD_DIM), lambda b, h, i, s: (b, i, N_HEADS + h)),
            pl.BlockSpec((1, full, HEAD_DIM), lambda b, h, i, s: (b, 0, h)),
            pl.BlockSpec((1, full, HEAD_DIM), lambda b, h, i, s: (b, 0, N_HEADS + h)),
            pl.BlockSpec((1, full, V_HEAD_DIM), lambda b, h, i, s: (b, 0, N_HEADS + h)),
            lspec, lspec, lspec, lspec,
            pl.BlockSpec((1, V_HEAD_DIM), lambda b, h, i, s: (0, 0)),
        ],
        out_specs=pl.BlockSpec((1, tq, V_HEAD_DIM), lambda b, h, i, s: (b, i, h)),
        scratch_shapes=[
            pltpu.VMEM((2, tq, 1), F32),
            pltpu.VMEM((2, tq, 1), F32),
            pltpu.VMEM((2, tq, V_HEAD_DIM), F32),
        ],
    )
    return pl.pallas_call(
        functools.partial(_attn_kernel, tq=tq, lambda_init=lambda_init),
        grid_spec=grid_spec,
        out_shape=jax.ShapeDtypeStruct((bsz, seq, N_HEADS * V_HEAD_DIM), BF16),
        compiler_params=_cparams("parallel", "parallel", "arbitrary"),
        name="diff_attention",
    )(slopes, q, q, kv, kv, kv, vec(lq1), vec(lk1), vec(lq2), vec(lk2), vec(subln_g))


def _router_kernel(x_ref, g_ref, r_ref, idx_ref, w_ref):
    u = _rms(x_ref[...], g_ref[...])
    logits = jnp.dot(u, r_ref[...], precision=lax.Precision.HIGHEST, preferred_element_type=F32)
    lane = lax.broadcasted_iota(jnp.int32, logits.shape, 1).astype(F32)
    lg = jnp.where(lane < N_EXPERTS, logits, -jnp.inf)
    m1 = jnp.max(lg, axis=-1, keepdims=True)
    i1 = jnp.min(jnp.where(lg == m1, lane, float(LANES)), axis=-1, keepdims=True)
    lg2 = jnp.where(lane == i1, -jnp.inf, lg)
    m2 = jnp.max(lg2, axis=-1, keepdims=True)
    i2 = jnp.min(jnp.where(lg2 == m2, lane, float(LANES)), axis=-1, keepdims=True)
    e2 = jnp.exp(m2 - m1)
    w1 = 1.0 / (1.0 + e2)
    w2 = e2 / (1.0 + e2)
    idx_ref[...] = jnp.where(lane == 0.0, i1, jnp.where(lane == 1.0, i2, 0.0)).astype(jnp.int32)
    w_ref[...] = jnp.where(lane == 0.0, w1, jnp.where(lane == 1.0, w2, 0.0))


def _router(x, g, router, *, tm):
    m, d = x.shape
    router_p = jnp.pad(router, ((0, 0), (0, LANES - N_EXPERTS)))
    out = pl.BlockSpec((tm, LANES), lambda i: (i, 0))
    return pl.pallas_call(
        _router_kernel,
        grid=(m // tm,),
        in_specs=[
            pl.BlockSpec((tm, d), lambda i: (i, 0)),
            pl.BlockSpec((1, d), lambda i: (0, 0)),
            pl.BlockSpec((d, LANES), lambda i: (0, 0)),
        ],
        out_specs=[out, out],
        out_shape=[jax.ShapeDtypeStruct((m, LANES), jnp.int32), jax.ShapeDtypeStruct((m, LANES), F32)],
        compiler_params=_cparams("parallel"),
        name="moe_router",
    )(x, g.reshape(1, d), router_p)


def _gather_norm_kernel(src_ref, nt_ref, h_hbm, g_ref, xs_ref, buf, sem, *, tm):
    i = pl.program_id(0)
    n_tiles = nt_ref[0]

    def issue(t, slot):
        def body(r, carry):
            tok = src_ref[t * tm + r]
            pltpu.make_async_copy(h_hbm.at[pl.ds(tok, 1), :], buf.at[slot, pl.ds(r, 1), :], sem.at[slot]).start()
            return carry
        lax.fori_loop(0, tm, body, 0, unroll=8)

    @pl.when(i == 0)
    def _():
        issue(0, 0)

    @pl.when(i + 1 < n_tiles)
    def _():
        issue(i + 1, (i + 1) & 1)

    @pl.when(i < n_tiles)
    def _():
        slot = i & 1
        pltpu.make_async_copy(h_hbm.at[pl.ds(0, tm), :], buf.at[slot], sem.at[slot]).wait()
        xs_ref[...] = _rms(buf[slot], g_ref[...]).astype(BF16)

    @pl.when(i >= n_tiles)
    def _():
        xs_ref[...] = jnp.zeros_like(xs_ref)


def _gather_norm(h, g, src_token, n_tiles, *, tm, max_tiles):
    _, d = h.shape
    grid_spec = pltpu.PrefetchScalarGridSpec(
        num_scalar_prefetch=2,
        grid=(max_tiles,),
        in_specs=[pl.BlockSpec(memory_space=pl.ANY), pl.BlockSpec((1, d), lambda i, s, n: (0, 0))],
        out_specs=pl.BlockSpec((tm, d), lambda i, s, n: (i, 0)),
        scratch_shapes=[pltpu.VMEM((2, tm, d), F32), pltpu.SemaphoreType.DMA((2,))],
    )
    return pl.pallas_call(
        functools.partial(_gather_norm_kernel, tm=tm),
        grid_spec=grid_spec,
        out_shape=jax.ShapeDtypeStruct((max_tiles * tm, d), BF16),
        compiler_params=_cparams("arbitrary"),
        name="moe_gather",
    )(src_token, n_tiles, h, g.reshape(1, d))


def _combine_kernel(dest_ref, o_hbm, h_ref, w_ref, g_ref, out_ref, buf, sem, *, tm):
    i = pl.program_id(0)
    n = pl.num_programs(0)

    def issue(t, slot):
        def body(r, carry):
            for k in range(TOP_K):
                row = dest_ref[(t * tm + r) * TOP_K + k]
                pltpu.make_async_copy(o_hbm.at[pl.ds(row, 1), :], buf.at[slot, pl.ds(k * tm + r, 1), :],
                                      sem.at[slot]).start()
            return carry
        lax.fori_loop(0, tm, body, 0, unroll=4)

    @pl.when(i == 0)
    def _():
        issue(0, 0)

    @pl.when(i + 1 < n)
    def _():
        issue(i + 1, (i + 1) & 1)

    slot = i & 1
    pltpu.make_async_copy(o_hbm.at[pl.ds(0, TOP_K * tm), :], buf.at[slot], sem.at[slot]).wait()
    w = w_ref[...]
    y = h_ref[...] + w[:, 0:1] * buf[slot, 0:tm, :] + w[:, 1:2] * buf[slot, tm:2 * tm, :]
    out_ref[...] = _rms(y, g_ref[...])


def _combine(o_sorted, h, wts, g, dest, *, tm):
    m, d = h.shape
    grid_spec = pltpu.PrefetchScalarGridSpec(
        num_scalar_prefetch=1,
        grid=(m // tm,),
        in_specs=[
            pl.BlockSpec(memory_space=pl.ANY),
            pl.BlockSpec((tm, d), lambda i, s: (i, 0)),
            pl.BlockSpec((tm, LANES), lambda i, s: (i, 0)),
            pl.BlockSpec((1, d), lambda i, s: (0, 0)),
        ],
        out_specs=pl.BlockSpec((tm, d), lambda i, s: (i, 0)),
        scratch_shapes=[pltpu.VMEM((2, TOP_K * tm, d), F32), pltpu.SemaphoreType.DMA((2,))],
    )
    return pl.pallas_call(
        functools.partial(_combine_kernel, tm=tm),
        grid_spec=grid_spec,
        out_shape=jax.ShapeDtypeStruct((m, d), F32),
        compiler_params=_cparams("arbitrary"),
        name="moe_combine",
    )(dest, o_sorted, h, wts, g.reshape(1, d))


def _moe_up_kernel(te_ref, first_ref, nt_ref, xs_ref, wg_ref, wu_ref, h_ref, wgb, wub):
    i = pl.program_id(1)

    @pl.when(first_ref[i] == 1)
    def _():
        wgb[...] = wg_ref[...].astype(BF16)
        wub[...] = wu_ref[...].astype(BF16)

    @pl.when(i < nt_ref[0])
    def _():
        x = xs_ref[...]
        h_ref[...] = (_silu(_bdot(x, wgb[...])) * _bdot(x, wub[...])).astype(BF16)

    @pl.when(i >= nt_ref[0])
    def _():
        h_ref[...] = jnp.zeros_like(h_ref)


def _moe_up(xs, w_gate, w_up, te, first, n_tiles, *, tm, tf, col0, ncols):
    npad, d = xs.shape
    max_tiles = npad // tm
    jb = col0 // tf
    wspec = pl.BlockSpec((None, d, tf), lambda j, i, te, fi, nt: (te[i], 0, jb + j))
    grid_spec = pltpu.PrefetchScalarGridSpec(
        num_scalar_prefetch=3,
        grid=(ncols // tf, max_tiles),
        in_specs=[pl.BlockSpec((tm, d), lambda j, i, te, fi, nt: (jnp.minimum(i, nt[0] - 1), 0)), wspec, wspec],
        out_specs=pl.BlockSpec((tm, tf), lambda j, i, te, fi, nt: (i, j)),
        scratch_shapes=[pltpu.VMEM((d, tf), BF16), pltpu.VMEM((d, tf), BF16)],
    )
    return pl.pallas_call(
        _moe_up_kernel,
        grid_spec=grid_spec,
        out_shape=jax.ShapeDtypeStruct((npad, ncols), BF16),
        compiler_params=_cparams("arbitrary", "arbitrary"),
        name="moe_up",
    )(te, first, n_tiles, xs, w_gate, w_up)


def _moe_down_kernel(te_ref, first_ref, nt_ref, hm_ref, ht_ref, wd_ref, wdt_ref, o_ref, wdb, wdtb):
    i = pl.program_id(1)

    @pl.when(first_ref[i] == 1)
    def _():
        wdb[...] = wd_ref[...].astype(BF16)
        wdtb[...] = wdt_ref[...].astype(BF16)

    @pl.when(i < nt_ref[0])
    def _():
        o_ref[...] = _bdot(hm_ref[...], wdb[...]) + _bdot(ht_ref[...], wdtb[...])

    @pl.when(i >= nt_ref[0])
    def _():
        o_ref[...] = jnp.zeros_like(o_ref)


def _moe_down(h_main, h_tail, w_down, te, first, n_tiles, *, tm, tn):
    npad = h_main.shape[0]
    d = w_down.shape[2]
    max_tiles = npad // tm
    tail_blk = FF_MAIN // FF_TAIL
    row = lambda j, i, te, fi, nt: (jnp.minimum(i, nt[0] - 1), 0)
    grid_spec = pltpu.PrefetchScalarGridSpec(
        num_scalar_prefetch=3,
        grid=(d // tn, max_tiles),
        in_specs=[
            pl.BlockSpec((tm, FF_MAIN), row),
            pl.BlockSpec((tm, FF_TAIL), row),
            pl.BlockSpec((None, FF_MAIN, tn), lambda j, i, te, fi, nt: (te[i], 0, j)),
            pl.BlockSpec((None, FF_TAIL, tn), lambda j, i, te, fi, nt: (te[i], tail_blk, j)),
        ],
        out_specs=pl.BlockSpec((tm, tn), lambda j, i, te, fi, nt: (i, j)),
        scratch_shapes=[pltpu.VMEM((FF_MAIN, tn), BF16), pltpu.VMEM((FF_TAIL, tn), BF16)],
    )
    return pl.pallas_call(
        _moe_down_kernel,
        grid_spec=grid_spec,
        out_shape=jax.ShapeDtypeStruct((npad, d), F32),
        compiler_params=_cparams("arbitrary", "arbitrary"),
        name="moe_down",
    )(te, first, n_tiles, h_main, h_tail, w_down, w_down)


def _moe_plan(expert_idx, *, tm, max_tiles):
    n = expert_idx.shape[0]
    flat = expert_idx.reshape(-1)
    onehot = (flat[:, None] == jnp.arange(N_EXPERTS, dtype=jnp.int32)[None, :]).astype(jnp.int32)
    incl = jnp.cumsum(onehot, axis=0)
    counts = incl[-1]
    rank = jnp.sum((incl - onehot) * onehot, axis=1)
    tiles_e = (counts + tm - 1) // tm
    tile_end = jnp.cumsum(tiles_e)
    row_start = (tile_end - tiles_e) * tm
    dest = jnp.sum(onehot * row_start[None, :], axis=1) + rank
    n_tiles = tile_end[-1:]
    tile_ids = jnp.arange(max_tiles, dtype=jnp.int32)
    te = jnp.sum((tile_ids[:, None] >= tile_end[None, :]).astype(jnp.int32), axis=1)
    last_e = jnp.sum((n_tiles - 1 >= tile_end).astype(jnp.int32))
    te = jnp.minimum(te, last_e)
    first = jnp.concatenate([jnp.ones((1,), jnp.int32), (te[1:] != te[:-1]).astype(jnp.int32)])
    token = jnp.arange(n * TOP_K, dtype=jnp.int32) // TOP_K
    src_token = jnp.zeros((max_tiles * tm,), jnp.int32).at[dest].set(token)
    return dest.astype(jnp.int32), src_token, te.astype(jnp.int32), first, n_tiles.astype(jnp.int32)


def kernel(x, meta_tokens, a_norm, a_w_in, a_conv_w, a_conv_b, a_w_rgate, a_b_rgate, a_w_igate, a_b_igate,
           a_lambda, a_w_out, kv_norm, w_kv, b_norm, b_w_q, b_lambda_q1, b_lambda_k1, b_lambda_q2, b_lambda_k2,
           b_subln, b_w_out, ffn_norm, ffn_w_gate, ffn_w_up, ffn_w_down, moe_router, moe_w_gate, moe_w_up,
           moe_w_down, final_norm):
    bsz, seq, d = x.shape
    full = N_META + seq
    tm0 = 688
    h = jnp.concatenate([jnp.broadcast_to(meta_tokens.astype(x.dtype)[None], (bsz, N_META, d)), x], axis=1)
    h = h.reshape(bsz * full, d)

    xg = _norm_matmul(h, a_norm[0], a_w_in[0], tm=tm0, tn=512, out_dtype=F32)
    y = _rglru(xg.reshape(bsz, full, 2 * d), a_conv_w[0], a_conv_b[0], a_w_rgate[0], a_b_rgate[0],
               a_w_igate[0], a_b_igate[0], a_lambda[0], c=256)
    h = _matmul_res(y.reshape(bsz * full, d), a_w_out[0], h, tm=tm0, tn=512)
    h = _ffn(h, ffn_norm[0], ffn_w_gate[0], ffn_w_up[0], ffn_w_down[0], tm=tm0, tf=256)

    kv = _norm_matmul(h, kv_norm, w_kv, tm=tm0, tn=512, out_dtype=BF16).reshape(bsz, full, -1)
    h = h.reshape(bsz, full, d)[:, N_META:].reshape(bsz * seq, d)

    lambda_init = 0.8 - 0.6 * math.exp(-0.3 * 1)
    q = _norm_matmul(h, b_norm[0], b_w_q[0], tm=512, tn=512, out_dtype=BF16, scale=HEAD_DIM ** -0.5)
    o = _diff_attention(q.reshape(bsz, seq, -1), kv, b_lambda_q1[0], b_lambda_k1[0], b_lambda_q2[0],
                        b_lambda_k2[0], b_subln[0], tq=256, lambda_init=lambda_init)
    h = _matmul_res(o.reshape(bsz * seq, -1), b_w_out[0], h, tm=512, tn=512)

    tm = 256
    n_tok = bsz * seq
    max_tiles = n_tok * TOP_K // tm + N_EXPERTS
    idx, wts = _router(h, ffn_norm[1], moe_router[0], tm=512)
    dest, src_token, te, first, n_tiles = _moe_plan(idx[:, :TOP_K], tm=tm, max_tiles=max_tiles)
    xs = _gather_norm(h, ffn_norm[1], src_token, n_tiles, tm=tm, max_tiles=max_tiles)
    hm = _moe_up(xs, moe_w_gate[0], moe_w_up[0], te, first, n_tiles, tm=tm, tf=768, col0=0, ncols=FF_MAIN)
    ht = _moe_up(xs, moe_w_gate[0], moe_w_up[0], te, first, n_tiles, tm=tm, tf=FF_TAIL, col0=FF_MAIN,
                 ncols=FF_TAIL)
    o_sorted = _moe_down(hm, ht, moe_w_down[0], te, first, n_tiles, tm=tm, tn=512)
    out = _combine(o_sorted, h, wts, final_norm, dest, tm=tm)
    return out.reshape(bsz, seq, d)
```

```python
import functools
import math

import jax
import jax.numpy as jnp
from jax import lax
from jax.experimental import pallas as pl
from jax.experimental.pallas import tpu as pltpu

F32 = jnp.float32
BF16 = jnp.bfloat16

D_MODEL = 2048
N_META = 16
CONV_WIDTH = 4
RGLRU_C = 8.0
RNN_BLOCK = 128
HEAD_DIM = 128
N_HEADS = 8
V_HEAD_DIM = 2 * HEAD_DIM
D_FF = 5504
N_EXPERTS = 8
TOP_K = 2
EPS = 1e-6

LANES = 128
FF_TAIL = LANES
FF_MAIN = D_FF - FF_TAIL
VMEM_LIMIT = 56 * 1024 * 1024


def _cparams(*sem):
    return pltpu.CompilerParams(dimension_semantics=sem, vmem_limit_bytes=VMEM_LIMIT)


def _rms(x, g):
    return x * lax.rsqrt(jnp.mean(x * x, axis=-1, keepdims=True) + EPS) * g


def _sigmoid(x):
    return 1.0 / (1.0 + jnp.exp(-x))


def _silu(x):
    return x * _sigmoid(x)


def _gelu_tanh(x):
    return 0.5 * x * (1.0 + jnp.tanh(math.sqrt(2.0 / math.pi) * (x + 0.044715 * (x * x * x))))


def _bdot(a, b):
    return jnp.dot(a, b, preferred_element_type=F32)


def _norm_matmul_kernel(x_ref, g_ref, w_ref, o_ref, xn_ref, *, scale):
    @pl.when(pl.program_id(1) == 0)
    def _():
        xn_ref[...] = _rms(x_ref[...], g_ref[...]).astype(BF16)

    acc = _bdot(xn_ref[...], w_ref[...].astype(BF16))
    if scale != 1.0:
        acc = acc * scale
    o_ref[...] = acc.astype(o_ref.dtype)


def _norm_matmul(x, g, w, *, tm, tn, out_dtype, scale=1.0):
    m, k = x.shape
    n = w.shape[1]
    return pl.pallas_call(
        functools.partial(_norm_matmul_kernel, scale=scale),
        grid=(m // tm, n // tn),
        in_specs=[
            pl.BlockSpec((tm, k), lambda i, j: (i, 0)),
            pl.BlockSpec((1, k), lambda i, j: (0, 0)),
            pl.BlockSpec((k, tn), lambda i, j: (0, j)),
        ],
        out_specs=pl.BlockSpec((tm, tn), lambda i, j: (i, j)),
        out_shape=jax.ShapeDtypeStruct((m, n), out_dtype),
        scratch_shapes=[pltpu.VMEM((tm, k), BF16)],
        compiler_params=_cparams("parallel", "arbitrary"),
        name="norm_matmul",
    )(x, g.reshape(1, k), w)


def _matmul_res_kernel(y_ref, w_ref, r_ref, o_ref):
    o_ref[...] = r_ref[...] + _bdot(y_ref[...], w_ref[...].astype(BF16))


def _matmul_res(y, w, res, *, tm, tn):
    m, k = y.shape
    n = w.shape[1]
    return pl.pallas_call(
        _matmul_res_kernel,
        grid=(m // tm, n // tn),
        in_specs=[
            pl.BlockSpec((tm, k), lambda i, j: (i, 0)),
            pl.BlockSpec((k, tn), lambda i, j: (0, j)),
            pl.BlockSpec((tm, tn), lambda i, j: (i, j)),
        ],
        out_specs=pl.BlockSpec((tm, tn), lambda i, j: (i, j)),
        out_shape=jax.ShapeDtypeStruct((m, n), F32),
        compiler_params=_cparams("parallel", "arbitrary"),
        name="matmul_res",
    )(y, w, res)


def _rglru_kernel(x_ref, g_ref, cw_ref, cb_ref, wr_ref, br_ref, wi_ref, bi_ref, lam_ref, y_ref, a_s, b_s):
    seq, c = a_s.shape
    x = x_ref[0]
    row = lax.broadcasted_iota(jnp.int32, (seq, c), 0)
    xc = jnp.broadcast_to(cb_ref[...], (seq, c))
    for k in range(CONV_WIDTH):
        shift = CONV_WIDTH - 1 - k
        xs = x if shift == 0 else jnp.where(row >= shift, pltpu.roll(x, shift, 0), 0.0)
        xc = xc + xs * cw_ref[k:k + 1, :]

    lam = lam_ref[...]
    softplus_neg_lam = jnp.maximum(-lam, 0.0) + jnp.log1p(jnp.exp(-jnp.abs(lam)))
    for gi in range(c // RNN_BLOCK):
        sl = slice(gi * RNN_BLOCK, (gi + 1) * RNN_BLOCK)
        xcg = xc[:, sl]
        xb = xcg.astype(BF16)
        r = _sigmoid(_bdot(xb, wr_ref[gi].astype(BF16)) + br_ref[:, sl])
        ig = _sigmoid(_bdot(xb, wi_ref[gi].astype(BF16)) + bi_ref[:, sl])
        log_a = (-RGLRU_C) * r * softplus_neg_lam[:, sl]
        a = jnp.exp(log_a)
        a_s[:, sl] = a
        b_s[:, sl] = jnp.sqrt(jnp.tanh(-log_a) * (1.0 + a * a)) * (ig * xcg)

    sub = lax.broadcasted_iota(jnp.int32, (8, c), 0)

    def tile(i, h_prev):
        r0 = pl.multiple_of(i * 8, 8)
        a = a_s[pl.ds(r0, 8), :]
        b = b_s[pl.ds(r0, 8), :]
        for s in (1, 2, 4):
            keep = sub >= s
            a_sh = jnp.where(keep, pltpu.roll(a, s, 0), 1.0)
            b_sh = jnp.where(keep, pltpu.roll(b, s, 0), 0.0)
            b = a * b_sh + b
            a = a * a_sh
        h = a * h_prev + b
        b_s[pl.ds(r0, 8), :] = h
        return jnp.broadcast_to(h[7:8, :], (8, c))

    lax.fori_loop(0, seq // 8, tile, jnp.zeros((8, c), F32))
    y_ref[0] = (b_s[...] * _gelu_tanh(g_ref[0])).astype(BF16)


def _rglru(xg, conv_w, conv_b, w_rg, b_rg, w_ig, b_ig, lam, *, c):
    bsz, seq, two_d = xg.shape
    d = two_d // 2
    nc = d // c
    gpb = c // RNN_BLOCK
    vec = lambda v: v.reshape(1, d)
    vspec = pl.BlockSpec((1, c), lambda b, j: (0, j))
    return pl.pallas_call(
        _rglru_kernel,
        grid=(bsz, nc),
        in_specs=[
            pl.BlockSpec((1, seq, c), lambda b, j: (b, 0, j)),
            pl.BlockSpec((1, seq, c), lambda b, j: (b, 0, nc + j)),
            pl.BlockSpec((CONV_WIDTH, c), lambda b, j: (0, j)),
            vspec,
            pl.BlockSpec((gpb, RNN_BLOCK, RNN_BLOCK), lambda b, j: (j, 0, 0)),
            vspec,
            pl.BlockSpec((gpb, RNN_BLOCK, RNN_BLOCK), lambda b, j: (j, 0, 0)),
            vspec,
            vspec,
        ],
        out_specs=pl.BlockSpec((1, seq, c), lambda b, j: (b, 0, j)),
        out_shape=jax.ShapeDtypeStruct((bsz, seq, d), BF16),
        scratch_shapes=[pltpu.VMEM((seq, c), F32), pltpu.VMEM((seq, c), F32)],
        compiler_params=_cparams("parallel", "parallel"),
        name="rglru",
    )(xg, xg, conv_w, vec(conv_b), w_rg, vec(b_rg), w_ig, vec(b_ig), vec(lam))


def _ffn_kernel(x_ref, g_ref, wg_ref, wu_ref, wd_ref, wgt_ref, wut_ref, wdt_ref, o_ref, xn_ref):
    j = pl.program_id(1)

    def contrib(wg, wu, wd):
        xn = xn_ref[...]
        h = _silu(_bdot(xn, wg[...].astype(BF16))) * _bdot(xn, wu[...].astype(BF16))
        return _bdot(h.astype(BF16), wd[...].astype(BF16))

    @pl.when(j == 0)
    def _():
        x = x_ref[...]
        xn_ref[...] = _rms(x, g_ref[...]).astype(BF16)
        o_ref[...] = x + contrib(wgt_ref, wut_ref, wdt_ref)

    o_ref[...] += contrib(wg_ref, wu_ref, wd_ref)


def _ffn(x, g, w_gate, w_up, w_down, *, tm, tf):
    m, d = x.shape
    tail_blk = FF_MAIN // FF_TAIL
    return pl.pallas_call(
        _ffn_kernel,
        grid=(m // tm, FF_MAIN // tf),
        in_specs=[
            pl.BlockSpec((tm, d), lambda i, j: (i, 0)),
            pl.BlockSpec((1, d), lambda i, j: (0, 0)),
            pl.BlockSpec((d, tf), lambda i, j: (0, j)),
            pl.BlockSpec((d, tf), lambda i, j: (0, j)),
            pl.BlockSpec((tf, d), lambda i, j: (j, 0)),
            pl.BlockSpec((d, FF_TAIL), lambda i, j: (0, tail_blk)),
            pl.BlockSpec((d, FF_TAIL), lambda i, j: (0, tail_blk)),
            pl.BlockSpec((FF_TAIL, d), lambda i, j: (tail_blk, 0)),
        ],
        out_specs=pl.BlockSpec((tm, d), lambda i, j: (i, 0)),
        out_shape=jax.ShapeDtypeStruct((m, d), F32),
        scratch_shapes=[pltpu.VMEM((tm, d), BF16)],
        compiler_params=_cparams("parallel", "arbitrary"),
        name="dense_ffn",
    )(x, g.reshape(1, d), w_gate, w_up, w_down, w_gate, w_up, w_down)


def _attn_kernel(slopes_ref, q1_ref, q2_ref, k1_ref, k2_ref, v_ref, lq1_ref, lk1_ref, lq2_ref, lk2_ref,
                 sg_ref, o_ref, *, tq, lambda_init):
    slope = slopes_ref[pl.program_id(1)]
    seq = q1_ref.shape[1]
    q_refs = (q1_ref, q2_ref)
    k_refs = (k1_ref, k2_ref)
    nt = (((1,), (1,)), ((), ()))
    lam = (jnp.exp(jnp.sum(lq1_ref[...] * lk1_ref[...], axis=-1, keepdims=True))
           - jnp.exp(jnp.sum(lq2_ref[...] * lk2_ref[...], axis=-1, keepdims=True)) + lambda_init)
    lane = lax.broadcasted_iota(jnp.int32, (1, LANES), 1)
    meta_mask = lane < N_META
    col = lax.broadcasted_iota(jnp.int32, (1, tq), 1)
    diag_mask = col <= lax.broadcasted_iota(jnp.int32, (tq, tq), 0)
    diag_bias = slope * col.astype(F32)

    for qi in range(seq // tq):
        r0 = qi * tq
        pieces = [(0, LANES, slope * (lane - (N_META + r0)).astype(F32), meta_mask),
                  (N_META + r0, tq, diag_bias, diag_mask)]
        if qi > 0:
            off_col = lax.broadcasted_iota(jnp.int32, (1, r0), 1)
            pieces.append((N_META, r0, slope * (off_col - r0).astype(F32), None))

        exps, scales = [], []
        for s in range(2):
            q = q_refs[s][0, r0:r0 + tq, :]
            scores = []
            for k0, rows, bias, mask in pieces:
                sc = lax.dot_general(q, k_refs[s][0, k0:k0 + rows, :], nt, preferred_element_type=F32) + bias
                scores.append(sc if mask is None else jnp.where(mask, sc, -jnp.inf))
            m = functools.reduce(jnp.maximum, [jnp.max(sc, axis=-1, keepdims=True) for sc in scores])
            e = [jnp.exp(sc - m) for sc in scores]
            l = functools.reduce(jnp.add, [jnp.sum(x, axis=-1, keepdims=True) for x in e])
            exps.append(e)
            scales.append(1.0 / l if s == 0 else lam / l)

        o = None
        for (k0, rows, _, _), e1, e2 in zip(pieces, exps[0], exps[1]):
            p = (e1 * scales[0] - e2 * scales[1]).astype(BF16)
            pv = _bdot(p, v_ref[0, k0:k0 + rows, :])
            o = pv if o is None else o + pv
        o_ref[0, r0:r0 + tq, :] = (_rms(o, sg_ref[...]) * (1.0 - lambda_init)).astype(o_ref.dtype)


def _diff_attention(q, kv, lq1, lk1, lq2, lk2, subln_g, *, tq, lambda_init):
    bsz, seq, _ = q.shape
    full = kv.shape[1]
    slopes = jnp.asarray([2.0 ** (-8.0 * (i + 1) / N_HEADS) for i in range(N_HEADS)], F32)
    vec = lambda v: v.reshape(1, -1)
    lspec = pl.BlockSpec((1, HEAD_DIM), lambda b, h, s: (0, 0))
    grid_spec = pltpu.PrefetchScalarGridSpec(
        num_scalar_prefetch=1,
        grid=(bsz, N_HEADS),
        in_specs=[
            pl.BlockSpec((1, seq, HEAD_DIM), lambda b, h, s: (b, 0, h)),
            pl.BlockSpec((1, seq, HEAD_DIM), lambda b, h, s: (b, 0, N_HEADS + h)),
            pl.BlockSpec((1, full, HEAD_DIM), lambda b, h, s: (b, 0, h)),
            pl.BlockSpec((1, full, HEAD_DIM), lambda b, h, s: (b, 0, N_HEADS + h)),
            pl.BlockSpec((1, full, V_HEAD_DIM), lambda b, h, s: (b, 0, N_HEADS + h)),
            lspec, lspec, lspec, lspec,
            pl.BlockSpec((1, V_HEAD_DIM), lambda b, h, s: (0, 0)),
        ],
        out_specs=pl.BlockSpec((1, seq, V_HEAD_DIM), lambda b, h, s: (b, 0, h)),
    )
    return pl.pallas_call(
        functools.partial(_attn_kernel, tq=tq, lambda_init=lambda_init),
        grid_spec=grid_spec,
        out_shape=jax.ShapeDtypeStruct((bsz, seq, N_HEADS * V_HEAD_DIM), BF16),
        compiler_params=_cparams("parallel", "parallel"),
        name="diff_attention",
    )(slopes, q, q, kv, kv, kv, vec(lq1), vec(lk1), vec(lq2), vec(lk2), vec(subln_g))


def _router_kernel(x_ref, g_ref, r_ref, idx_ref, w_ref):
    u = _rms(x_ref[...], g_ref[...])
    logits = jnp.dot(u, r_ref[...], precision=lax.Precision.HIGHEST, preferred_element_type=F32)
    lane = lax.broadcasted_iota(jnp.int32, logits.shape, 1).astype(F32)
    lg = jnp.where(lane < N_EXPERTS, logits, -jnp.inf)
    m1 = jnp.max(lg, axis=-1, keepdims=True)
    i1 = jnp.min(jnp.where(lg == m1, lane, float(LANES)), axis=-1, keepdims=True)
    lg2 = jnp.where(lane == i1, -jnp.inf, lg)
    m2 = jnp.max(lg2, axis=-1, keepdims=True)
    i2 = jnp.min(jnp.where(lg2 == m2, lane, float(LANES)), axis=-1, keepdims=True)
    e2 = jnp.exp(m2 - m1)
    w1 = 1.0 / (1.0 + e2)
    w2 = e2 / (1.0 + e2)
    idx_ref[...] = jnp.where(lane == 0.0, i1, jnp.where(lane == 1.0, i2, 0.0)).astype(jnp.int32)
    w_ref[...] = jnp.where(lane == 0.0, w1, jnp.where(lane == 1.0, w2, 0.0))


def _router(x, g, router, *, tm):
    m, d = x.shape
    router_p = jnp.pad(router, ((0, 0), (0, LANES - N_EXPERTS)))
    out = pl.BlockSpec((tm, LANES), lambda i: (i, 0))
    return pl.pallas_call(
        _router_kernel,
        grid=(m // tm,),
        in_specs=[
            pl.BlockSpec((tm, d), lambda i: (i, 0)),
            pl.BlockSpec((1, d), lambda i: (0, 0)),
            pl.BlockSpec((d, LANES), lambda i: (0, 0)),
        ],
        out_specs=[out, out],
        out_shape=[jax.ShapeDtypeStruct((m, LANES), jnp.int32), jax.ShapeDtypeStruct((m, LANES), F32)],
        compiler_params=_cparams("parallel"),
        name="moe_router",
    )(x, g.reshape(1, d), router_p)


def _gather_norm_kernel(src_ref, nt_ref, h_hbm, g_ref, xs_ref, buf, sem, *, tm):
    i = pl.program_id(0)
    n_tiles = nt_ref[0]

    def issue(t, slot):
        def body(r, carry):
            tok = src_ref[t * tm + r]
            pltpu.make_async_copy(h_hbm.at[pl.ds(tok, 1), :], buf.at[slot, pl.ds(r, 1), :], sem.at[slot]).start()
            return carry
        lax.fori_loop(0, tm, body, 0, unroll=8)

    @pl.when(i == 0)
    def _():
        issue(0, 0)

    @pl.when(i + 1 < n_tiles)
    def _():
        issue(i + 1, (i + 1) & 1)

    @pl.when(i < n_tiles)
    def _():
        slot = i & 1
        pltpu.make_async_copy(h_hbm.at[pl.ds(0, tm), :], buf.at[slot], sem.at[slot]).wait()
        xs_ref[...] = _rms(buf[slot], g_ref[...]).astype(BF16)

    @pl.when(i >= n_tiles)
    def _():
        xs_ref[...] = jnp.zeros_like(xs_ref)


def _gather_norm(h, g, src_token, n_tiles, *, tm, max_tiles):
    _, d = h.shape
    grid_spec = pltpu.PrefetchScalarGridSpec(
        num_scalar_prefetch=2,
        grid=(max_tiles,),
        in_specs=[pl.BlockSpec(memory_space=pl.ANY), pl.BlockSpec((1, d), lambda i, s, n: (0, 0))],
        out_specs=pl.BlockSpec((tm, d), lambda i, s, n: (i, 0)),
        scratch_shapes=[pltpu.VMEM((2, tm, d), F32), pltpu.SemaphoreType.DMA((2,))],
    )
    return pl.pallas_call(
        functools.partial(_gather_norm_kernel, tm=tm),
        grid_spec=grid_spec,
        out_shape=jax.ShapeDtypeStruct((max_tiles * tm, d), BF16),
        compiler_params=_cparams("arbitrary"),
        name="moe_gather",
    )(src_token, n_tiles, h, g.reshape(1, d))


def _combine_kernel(dest_ref, o_hbm, h_ref, w_ref, g_ref, out_ref, buf, sem, *, tm):
    i = pl.program_id(0)
    n = pl.num_programs(0)

    def issue(t, slot):
        def body(r, carry):
            for k in range(TOP_K):
                row = dest_ref[(t * tm + r) * TOP_K + k]
                pltpu.make_async_copy(o_hbm.at[pl.ds(row, 1), :], buf.at[slot, pl.ds(k * tm + r, 1), :],
                                      sem.at[slot]).start()
            return carry
        lax.fori_loop(0, tm, body, 0, unroll=4)

    @pl.when(i == 0)
    def _():
        issue(0, 0)

    @pl.when(i + 1 < n)
    def _():
        issue(i + 1, (i + 1) & 1)

    slot = i & 1
    pltpu.make_async_copy(o_hbm.at[pl.ds(0, TOP_K * tm), :], buf.at[slot], sem.at[slot]).wait()
    w = w_ref[...]
    y = h_ref[...] + w[:, 0:1] * buf[slot, 0:tm, :] + w[:, 1:2] * buf[slot, tm:2 * tm, :]
    out_ref[...] = _rms(y, g_ref[...])


def _combine(o_sorted, h, wts, g, dest, *, tm):
    m, d = h.shape
    grid_spec = pltpu.PrefetchScalarGridSpec(
        num_scalar_prefetch=1,
        grid=(m // tm,),
        in_specs=[
            pl.BlockSpec(memory_space=pl.ANY),
            pl.BlockSpec((tm, d), lambda i, s: (i, 0)),
            pl.BlockSpec((tm, LANES), lambda i, s: (i, 0)),
            pl.BlockSpec((1, d), lambda i, s: (0, 0)),
        ],
        out_specs=pl.BlockSpec((tm, d), lambda i, s: (i, 0)),
        scratch_shapes=[pltpu.VMEM((2, TOP_K * tm, d), F32), pltpu.SemaphoreType.DMA((2,))],
    )
    return pl.pallas_call(
        functools.partial(_combine_kernel, tm=tm),
        grid_spec=grid_spec,
        out_shape=jax.ShapeDtypeStruct((m, d), F32),
        compiler_params=_cparams("arbitrary"),
        name="moe_combine",
    )(dest, o_sorted, h, wts, g.reshape(1, d))


def _moe_up_kernel(te_ref, first_ref, nt_ref, xs_ref, wg_ref, wu_ref, h_ref, wgb, wub):
    i = pl.program_id(1)

    @pl.when(first_ref[i] == 1)
    def _():
        wgb[...] = wg_ref[...].astype(BF16)
        wub[...] = wu_ref[...].astype(BF16)

    @pl.when(i < nt_ref[0])
    def _():
        x = xs_ref[...]
        h_ref[...] = (_silu(_bdot(x, wgb[...])) * _bdot(x, wub[...])).astype(BF16)

    @pl.when(i >= nt_ref[0])
    def _():
        h_ref[...] = jnp.zeros_like(h_ref)


def _moe_up(xs, w_gate, w_up, te, first, n_tiles, *, tm, tf, col0, ncols):
    npad, d = xs.shape
    max_tiles = npad // tm
    jb = col0 // tf
    wspec = pl.BlockSpec((None, d, tf), lambda j, i, te, fi, nt: (te[i], 0, jb + j))
    grid_spec = pltpu.PrefetchScalarGridSpec(
        num_scalar_prefetch=3,
        grid=(ncols // tf, max_tiles),
        in_specs=[pl.BlockSpec((tm, d), lambda j, i, te, fi, nt: (jnp.minimum(i, nt[0] - 1), 0)), wspec, wspec],
        out_specs=pl.BlockSpec((tm, tf), lambda j, i, te, fi, nt: (i, j)),
        scratch_shapes=[pltpu.VMEM((d, tf), BF16), pltpu.VMEM((d, tf), BF16)],
    )
    return pl.pallas_call(
        _moe_up_kernel,
        grid_spec=grid_spec,
        out_shape=jax.ShapeDtypeStruct((npad, ncols), BF16),
        compiler_params=_cparams("arbitrary", "arbitrary"),
        name="moe_up",
    )(te, first, n_tiles, xs, w_gate, w_up)


def _moe_down_kernel(te_ref, first_ref, nt_ref, hm_ref, ht_ref, wd_ref, wdt_ref, o_ref, wdb, wdtb):
    i = pl.program_id(1)

    @pl.when(first_ref[i] == 1)
    def _():
        wdb[...] = wd_ref[...].astype(BF16)
        wdtb[...] = wdt_ref[...].astype(BF16)

    @pl.when(i < nt_ref[0])
    def _():
        o_ref[...] = _bdot(hm_ref[...], wdb[...]) + _bdot(ht_ref[...], wdtb[...])

    @pl.when(i >= nt_ref[0])
    def _():
        o_ref[...] = jnp.zeros_like(o_ref)


def _moe_down(h_main, h_tail, w_down, te, first, n_tiles, *, tm, tn):
    npad = h_main.shape[0]
    d = w_down.shape[2]
    max_tiles = npad // tm
    tail_blk = FF_MAIN // FF_TAIL
    row = lambda j, i, te, fi, nt: (jnp.minimum(i, nt[0] - 1), 0)
    grid_spec = pltpu.PrefetchScalarGridSpec(
        num_scalar_prefetch=3,
        grid=(d // tn, max_tiles),
        in_specs=[
            pl.BlockSpec((tm, FF_MAIN), row),
            pl.BlockSpec((tm, FF_TAIL), row),
            pl.BlockSpec((None, FF_MAIN, tn), lambda j, i, te, fi, nt: (te[i], 0, j)),
            pl.BlockSpec((None, FF_TAIL, tn), lambda j, i, te, fi, nt: (te[i], tail_blk, j)),
        ],
        out_specs=pl.BlockSpec((tm, tn), lambda j, i, te, fi, nt: (i, j)),
        scratch_shapes=[pltpu.VMEM((FF_MAIN, tn), BF16), pltpu.VMEM((FF_TAIL, tn), BF16)],
    )
    return pl.pallas_call(
        _moe_down_kernel,
        grid_spec=grid_spec,
        out_shape=jax.ShapeDtypeStruct((npad, d), F32),
        compiler_params=_cparams("arbitrary", "arbitrary"),
        name="moe_down",
    )(te, first, n_tiles, h_main, h_tail, w_down, w_down)


def _moe_plan(expert_idx, *, tm, max_tiles):
    n = expert_idx.shape[0]
    flat = expert_idx.reshape(-1)
    onehot = (flat[:, None] == jnp.arange(N_EXPERTS, dtype=jnp.int32)[None, :]).astype(jnp.int32)
    incl = jnp.cumsum(onehot, axis=0)
    counts = incl[-1]
    rank = jnp.sum((incl - onehot) * onehot, axis=1)
    tiles_e = (counts + tm - 1) // tm
    tile_end = jnp.cumsum(tiles_e)
    row_start = (tile_end - tiles_e) * tm
    dest = jnp.sum(onehot * row_start[None, :], axis=1) + rank
    n_tiles = tile_end[-1:]
    tile_ids = jnp.arange(max_tiles, dtype=jnp.int32)
    te = jnp.sum((tile_ids[:, None] >= tile_end[None, :]).astype(jnp.int32), axis=1)
    last_e = jnp.sum((n_tiles - 1 >= tile_end).astype(jnp.int32))
    te = jnp.minimum(te, last_e)
    first = jnp.concatenate([jnp.ones((1,), jnp.int32), (te[1:] != te[:-1]).astype(jnp.int32)])
    token = jnp.arange(n * TOP_K, dtype=jnp.int32) // TOP_K
    src_token = jnp.zeros((max_tiles * tm,), jnp.int32).at[dest].set(token)
    return dest.astype(jnp.int32), src_token, te.astype(jnp.int32), first, n_tiles.astype(jnp.int32)


def kernel(x, meta_tokens, a_norm, a_w_in, a_conv_w, a_conv_b, a_w_rgate, a_b_rgate, a_w_igate, a_b_igate,
           a_lambda, a_w_out, kv_norm, w_kv, b_norm, b_w_q, b_lambda_q1, b_lambda_k1, b_lambda_q2, b_lambda_k2,
           b_subln, b_w_out, ffn_norm, ffn_w_gate, ffn_w_up, ffn_w_down, moe_router, moe_w_gate, moe_w_up,
           moe_w_down, final_norm):
    bsz, seq, d = x.shape
    full = N_META + seq
    tm0 = 688
    tm0_big = 2 * tm0
    tm1_big = 1024
    h = jnp.concatenate([jnp.broadcast_to(meta_tokens.astype(x.dtype)[None], (bsz, N_META, d)), x], axis=1)
    h = h.reshape(bsz * full, d)

    xg = _norm_matmul(h, a_norm[0], a_w_in[0], tm=tm0_big, tn=512, out_dtype=F32)
    y = _rglru(xg.reshape(bsz, full, 2 * d), a_conv_w[0], a_conv_b[0], a_w_rgate[0], a_b_rgate[0],
               a_w_igate[0], a_b_igate[0], a_lambda[0], c=256)
    h = _matmul_res(y.reshape(bsz * full, d), a_w_out[0], h, tm=tm0_big, tn=512)
    h = _ffn(h, ffn_norm[0], ffn_w_gate[0], ffn_w_up[0], ffn_w_down[0], tm=tm0, tf=256)

    kv = _norm_matmul(h, kv_norm, w_kv, tm=tm0_big, tn=512, out_dtype=BF16).reshape(bsz, full, -1)
    h = h.reshape(bsz, full, d)[:, N_META:].reshape(bsz * seq, d)

    lambda_init = 0.8 - 0.6 * math.exp(-0.3 * 1)
    q = _norm_matmul(h, b_norm[0], b_w_q[0], tm=tm1_big, tn=512, out_dtype=BF16, scale=HEAD_DIM ** -0.5)
    o = _diff_attention(q.reshape(bsz, seq, -1), kv, b_lambda_q1[0], b_lambda_k1[0], b_lambda_q2[0],
                        b_lambda_k2[0], b_subln[0], tq=256, lambda_init=lambda_init)
    h = _matmul_res(o.reshape(bsz * seq, -1), b_w_out[0], h, tm=tm1_big, tn=512)

    tm = 256
    n_tok = bsz * seq
    max_tiles = n_tok * TOP_K // tm + N_EXPERTS
    idx, wts = _router(h, ffn_norm[1], moe_router[0], tm=512)
    dest, src_token, te, first, n_tiles = _moe_plan(idx[:, :TOP_K], tm=tm, max_tiles=max_tiles)
    xs = _gather_norm(h, ffn_norm[1], src_token, n_tiles, tm=tm, max_tiles=max_tiles)
    hm = _moe_up(xs, moe_w_gate[0], moe_w_up[0], te, first, n_tiles, tm=tm, tf=768, col0=0, ncols=FF_MAIN)
    ht = _moe_up(xs, moe_w_gate[0], moe_w_up[0], te, first, n_tiles, tm=tm, tf=FF_TAIL, col0=FF_MAIN,
                 ncols=FF_TAIL)
    o_sorted = _moe_down(hm, ht, moe_w_down[0], te, first, n_tiles, tm=tm, tn=512)
    out = _combine(o_sorted, h, wts, final_norm, dest, tm=tm)
    return out.reshape(bsz, seq, d)
```

```python
import functools
import math

import jax
import jax.numpy as jnp
from jax import lax
from jax.experimental import pallas as pl
from jax.experimental.pallas import tpu as pltpu

F32 = jnp.float32
BF16 = jnp.bfloat16

D_MODEL = 2048
N_META = 16
CONV_WIDTH = 4
RGLRU_C = 8.0
RNN_BLOCK = 128
HEAD_DIM = 128
N_HEADS = 8
V_HEAD_DIM = 2 * HEAD_DIM
D_FF = 5504
N_EXPERTS = 8
TOP_K = 2
EPS = 1e-6

LANES = 128
FF_TAIL = LANES
FF_MAIN = D_FF - FF_TAIL
VMEM_LIMIT = 56 * 1024 * 1024
MOE_TILE = 512
MOE_SUB = 256


def _cparams(*sem):
    return pltpu.CompilerParams(dimension_semantics=sem, vmem_limit_bytes=VMEM_LIMIT)


def _rms(x, g):
    return x * lax.rsqrt(jnp.mean(x * x, axis=-1, keepdims=True) + EPS) * g


def _sigmoid(x):
    return 1.0 / (1.0 + jnp.exp(-x))


def _silu(x):
    return x * _sigmoid(x)


def _gelu_tanh(x):
    return 0.5 * x * (1.0 + jnp.tanh(math.sqrt(2.0 / math.pi) * (x + 0.044715 * (x * x * x))))


def _bdot(a, b):
    return jnp.dot(a, b, preferred_element_type=F32)


def _norm_matmul_kernel(x_ref, g_ref, w_ref, o_ref, xn_ref, *, scale):
    @pl.when(pl.program_id(1) == 0)
    def _():
        xn_ref[...] = _rms(x_ref[...], g_ref[...]).astype(BF16)

    acc = _bdot(xn_ref[...], w_ref[...].astype(BF16))
    if scale != 1.0:
        acc = acc * scale
    o_ref[...] = acc.astype(o_ref.dtype)


def _norm_matmul(x, g, w, *, tm, tn, out_dtype, scale=1.0):
    m, k = x.shape
    n = w.shape[1]
    return pl.pallas_call(
        functools.partial(_norm_matmul_kernel, scale=scale),
        grid=(m // tm, n // tn),
        in_specs=[
            pl.BlockSpec((tm, k), lambda i, j: (i, 0)),
            pl.BlockSpec((1, k), lambda i, j: (0, 0)),
            pl.BlockSpec((k, tn), lambda i, j: (0, j)),
        ],
        out_specs=pl.BlockSpec((tm, tn), lambda i, j: (i, j)),
        out_shape=jax.ShapeDtypeStruct((m, n), out_dtype),
        scratch_shapes=[pltpu.VMEM((tm, k), BF16)],
        compiler_params=_cparams("parallel", "arbitrary"),
        name="norm_matmul",
    )(x, g.reshape(1, k), w)


def _matmul_res_kernel(y_ref, w_ref, r_ref, o_ref):
    o_ref[...] = r_ref[...] + _bdot(y_ref[...], w_ref[...].astype(BF16))


def _matmul_res(y, w, res, *, tm, tn):
    m, k = y.shape
    n = w.shape[1]
    return pl.pallas_call(
        _matmul_res_kernel,
        grid=(m // tm, n // tn),
        in_specs=[
            pl.BlockSpec((tm, k), lambda i, j: (i, 0)),
            pl.BlockSpec((k, tn), lambda i, j: (0, j)),
            pl.BlockSpec((tm, tn), lambda i, j: (i, j)),
        ],
        out_specs=pl.BlockSpec((tm, tn), lambda i, j: (i, j)),
        out_shape=jax.ShapeDtypeStruct((m, n), F32),
        compiler_params=_cparams("parallel", "arbitrary"),
        name="matmul_res",
    )(y, w, res)


def _rglru_kernel(x_ref, g_ref, cw_ref, cb_ref, wr_ref, br_ref, wi_ref, bi_ref, lam_ref, y_ref, a_s, b_s):
    seq, c = a_s.shape
    x = x_ref[0]
    row = lax.broadcasted_iota(jnp.int32, (seq, c), 0)
    xc = jnp.broadcast_to(cb_ref[...], (seq, c))
    for k in range(CONV_WIDTH):
        shift = CONV_WIDTH - 1 - k
        xs = x if shift == 0 else jnp.where(row >= shift, pltpu.roll(x, shift, 0), 0.0)
        xc = xc + xs * cw_ref[k:k + 1, :]

    lam = lam_ref[...]
    softplus_neg_lam = jnp.maximum(-lam, 0.0) + jnp.log1p(jnp.exp(-jnp.abs(lam)))
    for gi in range(c // RNN_BLOCK):
        sl = slice(gi * RNN_BLOCK, (gi + 1) * RNN_BLOCK)
        xcg = xc[:, sl]
        xb = xcg.astype(BF16)
        r = _sigmoid(_bdot(xb, wr_ref[gi].astype(BF16)) + br_ref[:, sl])
        ig = _sigmoid(_bdot(xb, wi_ref[gi].astype(BF16)) + bi_ref[:, sl])
        log_a = (-RGLRU_C) * r * softplus_neg_lam[:, sl]
        a = jnp.exp(log_a)
        a_s[:, sl] = a
        b_s[:, sl] = jnp.sqrt(jnp.tanh(-log_a) * (1.0 + a * a)) * (ig * xcg)

    sub = lax.broadcasted_iota(jnp.int32, (8, c), 0)

    def tile(i, h_prev):
        r0 = pl.multiple_of(i * 8, 8)
        a = a_s[pl.ds(r0, 8), :]
        b = b_s[pl.ds(r0, 8), :]
        for s in (1, 2, 4):
            keep = sub >= s
            a_sh = jnp.where(keep, pltpu.roll(a, s, 0), 1.0)
            b_sh = jnp.where(keep, pltpu.roll(b, s, 0), 0.0)
            b = a * b_sh + b
            a = a * a_sh
        h = a * h_prev + b
        b_s[pl.ds(r0, 8), :] = h
        return jnp.broadcast_to(h[7:8, :], (8, c))

    lax.fori_loop(0, seq // 8, tile, jnp.zeros((8, c), F32))
    y_ref[0] = (b_s[...] * _gelu_tanh(g_ref[0])).astype(BF16)


def _rglru(xg, conv_w, conv_b, w_rg, b_rg, w_ig, b_ig, lam, *, c):
    bsz, seq, two_d = xg.shape
    d = two_d // 2
    nc = d // c
    gpb = c // RNN_BLOCK
    vec = lambda v: v.reshape(1, d)
    vspec = pl.BlockSpec((1, c), lambda b, j: (0, j))
    return pl.pallas_call(
        _rglru_kernel,
        grid=(bsz, nc),
        in_specs=[
            pl.BlockSpec((1, seq, c), lambda b, j: (b, 0, j)),
            pl.BlockSpec((1, seq, c), lambda b, j: (b, 0, nc + j)),
            pl.BlockSpec((CONV_WIDTH, c), lambda b, j: (0, j)),
            vspec,
            pl.BlockSpec((gpb, RNN_BLOCK, RNN_BLOCK), lambda b, j: (j, 0, 0)),
            vspec,
            pl.BlockSpec((gpb, RNN_BLOCK, RNN_BLOCK), lambda b, j: (j, 0, 0)),
            vspec,
            vspec,
        ],
        out_specs=pl.BlockSpec((1, seq, c), lambda b, j: (b, 0, j)),
        out_shape=jax.ShapeDtypeStruct((bsz, seq, d), BF16),
        scratch_shapes=[pltpu.VMEM((seq, c), F32), pltpu.VMEM((seq, c), F32)],
        compiler_params=_cparams("parallel", "parallel"),
        name="rglru",
    )(xg, xg, conv_w, vec(conv_b), w_rg, vec(b_rg), w_ig, vec(b_ig), vec(lam))


def _ffn_kernel(x_ref, g_ref, wg_ref, wu_ref, wd_ref, wgt_ref, wut_ref, wdt_ref, o_ref, xn_ref):
    j = pl.program_id(1)

    def contrib(wg, wu, wd):
        xn = xn_ref[...]
        h = _silu(_bdot(xn, wg[...].astype(BF16))) * _bdot(xn, wu[...].astype(BF16))
        return _bdot(h.astype(BF16), wd[...].astype(BF16))

    @pl.when(j == 0)
    def _():
        x = x_ref[...]
        xn_ref[...] = _rms(x, g_ref[...]).astype(BF16)
        o_ref[...] = x + contrib(wgt_ref, wut_ref, wdt_ref)

    o_ref[...] += contrib(wg_ref, wu_ref, wd_ref)


def _ffn(x, g, w_gate, w_up, w_down, *, tm, tf):
    m, d = x.shape
    tail_blk = FF_MAIN // FF_TAIL
    return pl.pallas_call(
        _ffn_kernel,
        grid=(m // tm, FF_MAIN // tf),
        in_specs=[
            pl.BlockSpec((tm, d), lambda i, j: (i, 0)),
            pl.BlockSpec((1, d), lambda i, j: (0, 0)),
            pl.BlockSpec((d, tf), lambda i, j: (0, j)),
            pl.BlockSpec((d, tf), lambda i, j: (0, j)),
            pl.BlockSpec((tf, d), lambda i, j: (j, 0)),
            pl.BlockSpec((d, FF_TAIL), lambda i, j: (0, tail_blk)),
            pl.BlockSpec((d, FF_TAIL), lambda i, j: (0, tail_blk)),
            pl.BlockSpec((FF_TAIL, d), lambda i, j: (tail_blk, 0)),
        ],
        out_specs=pl.BlockSpec((tm, d), lambda i, j: (i, 0)),
        out_shape=jax.ShapeDtypeStruct((m, d), F32),
        scratch_shapes=[pltpu.VMEM((tm, d), BF16)],
        compiler_params=_cparams("parallel", "arbitrary"),
        name="dense_ffn",
    )(x, g.reshape(1, d), w_gate, w_up, w_down, w_gate, w_up, w_down)


def _attn_kernel(slopes_ref, q1_ref, q2_ref, k1_ref, k2_ref, v_ref, lq1_ref, lk1_ref, lq2_ref, lk2_ref,
                 sg_ref, o_ref, *, tq, lambda_init):
    slope = slopes_ref[pl.program_id(1)]
    seq = q1_ref.shape[1]
    q_refs = (q1_ref, q2_ref)
    k_refs = (k1_ref, k2_ref)
    nt = (((1,), (1,)), ((), ()))
    lam = (jnp.exp(jnp.sum(lq1_ref[...] * lk1_ref[...], axis=-1, keepdims=True))
           - jnp.exp(jnp.sum(lq2_ref[...] * lk2_ref[...], axis=-1, keepdims=True)) + lambda_init)
    lane = lax.broadcasted_iota(jnp.int32, (1, LANES), 1)
    meta_mask = lane < N_META
    col = lax.broadcasted_iota(jnp.int32, (1, tq), 1)
    diag_mask = col <= lax.broadcasted_iota(jnp.int32, (tq, tq), 0)
    diag_bias = slope * col.astype(F32)

    for qi in range(seq // tq):
        r0 = qi * tq
        pieces = [(0, LANES, slope * (lane - (N_META + r0)).astype(F32), meta_mask),
                  (N_META + r0, tq, diag_bias, diag_mask)]
        if qi > 0:
            off_col = lax.broadcasted_iota(jnp.int32, (1, r0), 1)
            pieces.append((N_META, r0, slope * (off_col - r0).astype(F32), None))

        exps, scales = [], []
        for s in range(2):
            q = q_refs[s][0, r0:r0 + tq, :]
            scores = []
            for k0, rows, bias, mask in pieces:
                sc = lax.dot_general(q, k_refs[s][0, k0:k0 + rows, :], nt, preferred_element_type=F32) + bias
                scores.append(sc if mask is None else jnp.where(mask, sc, -jnp.inf))
            m = functools.reduce(jnp.maximum, [jnp.max(sc, axis=-1, keepdims=True) for sc in scores])
            e = [jnp.exp(sc - m) for sc in scores]
            l = functools.reduce(jnp.add, [jnp.sum(x, axis=-1, keepdims=True) for x in e])
            exps.append(e)
            scales.append(1.0 / l if s == 0 else lam / l)

        o = None
        for (k0, rows, _, _), e1, e2 in zip(pieces, exps[0], exps[1]):
            p = (e1 * scales[0] - e2 * scales[1]).astype(BF16)
            pv = _bdot(p, v_ref[0, k0:k0 + rows, :])
            o = pv if o is None else o + pv
        o_ref[0, r0:r0 + tq, :] = (_rms(o, sg_ref[...]) * (1.0 - lambda_init)).astype(o_ref.dtype)


def _diff_attention(q, kv, lq1, lk1, lq2, lk2, subln_g, *, tq, lambda_init):
    bsz, seq, _ = q.shape
    full = kv.shape[1]
    slopes = jnp.asarray([2.0 ** (-8.0 * (i + 1) / N_HEADS) for i in range(N_HEADS)], F32)
    vec = lambda v: v.reshape(1, -1)
    lspec = pl.BlockSpec((1, HEAD_DIM), lambda b, h, s: (0, 0))
    grid_spec = pltpu.PrefetchScalarGridSpec(
        num_scalar_prefetch=1,
        grid=(bsz, N_HEADS),
        in_specs=[
            pl.BlockSpec((1, seq, HEAD_DIM), lambda b, h, s: (b, 0, h)),
            pl.BlockSpec((1, seq, HEAD_DIM), lambda b, h, s: (b, 0, N_HEADS + h)),
            pl.BlockSpec((1, full, HEAD_DIM), lambda b, h, s: (b, 0, h)),
            pl.BlockSpec((1, full, HEAD_DIM), lambda b, h, s: (b, 0, N_HEADS + h)),
            pl.BlockSpec((1, full, V_HEAD_DIM), lambda b, h, s: (b, 0, N_HEADS + h)),
            lspec, lspec, lspec, lspec,
            pl.BlockSpec((1, V_HEAD_DIM), lambda b, h, s: (0, 0)),
        ],
        out_specs=pl.BlockSpec((1, seq, V_HEAD_DIM), lambda b, h, s: (b, 0, h)),
    )
    return pl.pallas_call(
        functools.partial(_attn_kernel, tq=tq, lambda_init=lambda_init),
        grid_spec=grid_spec,
        out_shape=jax.ShapeDtypeStruct((bsz, seq, N_HEADS * V_HEAD_DIM), BF16),
        compiler_params=_cparams("parallel", "parallel"),
        name="diff_attention",
    )(slopes, q, q, kv, kv, kv, vec(lq1), vec(lk1), vec(lq2), vec(lk2), vec(subln_g))


def _router_kernel(x_ref, g_ref, r_ref, idx_ref, w_ref):
    u = _rms(x_ref[...], g_ref[...])
    logits = jnp.dot(u, r_ref[...], precision=lax.Precision.HIGHEST, preferred_element_type=F32)
    lane = lax.broadcasted_iota(jnp.int32, logits.shape, 1).astype(F32)
    lg = jnp.where(lane < N_EXPERTS, logits, -jnp.inf)
    m1 = jnp.max(lg, axis=-1, keepdims=True)
    i1 = jnp.min(jnp.where(lg == m1, lane, float(LANES)), axis=-1, keepdims=True)
    lg2 = jnp.where(lane == i1, -jnp.inf, lg)
    m2 = jnp.max(lg2, axis=-1, keepdims=True)
    i2 = jnp.min(jnp.where(lg2 == m2, lane, float(LANES)), axis=-1, keepdims=True)
    e2 = jnp.exp(m2 - m1)
    w1 = 1.0 / (1.0 + e2)
    w2 = e2 / (1.0 + e2)
    idx_ref[...] = jnp.where(lane == 0.0, i1, jnp.where(lane == 1.0, i2, 0.0)).astype(jnp.int32)
    w_ref[...] = jnp.where(lane == 0.0, w1, jnp.where(lane == 1.0, w2, 0.0))


def _router(x, g, router, *, tm):
    m, d = x.shape
    router_p = jnp.pad(router, ((0, 0), (0, LANES - N_EXPERTS)))
    out = pl.BlockSpec((tm, LANES), lambda i: (i, 0))
    return pl.pallas_call(
        _router_kernel,
        grid=(m // tm,),
        in_specs=[
            pl.BlockSpec((tm, d), lambda i: (i, 0)),
            pl.BlockSpec((1, d), lambda i: (0, 0)),
            pl.BlockSpec((d, LANES), lambda i: (0, 0)),
        ],
        out_specs=[out, out],
        out_shape=[jax.ShapeDtypeStruct((m, LANES), jnp.int32), jax.ShapeDtypeStruct((m, LANES), F32)],
        compiler_params=_cparams("parallel"),
        name="moe_router",
    )(x, g.reshape(1, d), router_p)


def _gather_norm_kernel(src_ref, nsub_ref, h_hbm, g_ref, xs_ref, buf, sem, *, tile, sub):
    i = pl.program_id(0)
    unroll = 8

    def issue(t, slot):
        def body(blk, carry):
            for u in range(unroll):
                r = blk * unroll + u
                tok = src_ref[t * tile + r]
                pltpu.make_async_copy(h_hbm.at[pl.ds(tok, 1), :], buf.at[slot, pl.ds(r, 1), :],
                                      sem.at[slot]).start(priority=u % 2)
            return carry
        lax.fori_loop(0, nsub_ref[t] * (sub // unroll), body, 0)

    @pl.when(i == 0)
    def _():
        issue(0, 0)

    @pl.when(i + 1 < pl.num_programs(0))
    def _():
        issue(i + 1, (i + 1) & 1)

    slot = i & 1
    ns = nsub_ref[i]
    for s in range(tile // sub):
        @pl.when(s < ns)
        def _():
            rows = pl.ds(s * sub, sub)
            pltpu.make_async_copy(h_hbm.at[rows, :], buf.at[slot, rows, :], sem.at[slot]).wait()

    for s in range(tile // sub):
        rows = pl.ds(s * sub, sub)

        @pl.when(s < ns)
        def _():
            xs_ref[rows, :] = _rms(buf[slot, rows, :], g_ref[...]).astype(BF16)

        @pl.when(s >= ns)
        def _():
            xs_ref[rows, :] = jnp.zeros((sub, xs_ref.shape[1]), BF16)


def _gather_norm(h, g, src_token, nsub, *, tile, sub):
    _, d = h.shape
    max_tiles = nsub.shape[0]
    grid_spec = pltpu.PrefetchScalarGridSpec(
        num_scalar_prefetch=2,
        grid=(max_tiles,),
        in_specs=[pl.BlockSpec(memory_space=pl.ANY), pl.BlockSpec((1, d), lambda i, s, n: (0, 0))],
        out_specs=pl.BlockSpec((tile, d), lambda i, s, n: (i, 0)),
        scratch_shapes=[pltpu.VMEM((2, tile, d), F32), pltpu.SemaphoreType.DMA((2,))],
    )
    return pl.pallas_call(
        functools.partial(_gather_norm_kernel, tile=tile, sub=sub),
        grid_spec=grid_spec,
        out_shape=jax.ShapeDtypeStruct((max_tiles * tile, d), BF16),
        compiler_params=_cparams("arbitrary"),
        name="moe_gather",
    )(src_token, nsub, h, g.reshape(1, d))


def _combine_kernel(dest_ref, o_hbm, h_ref, w_ref, g_ref, out_ref, buf, sem, *, tm):
    i = pl.program_id(0)
    n = pl.num_programs(0)

    def issue(t, slot):
        def body(r, carry):
            for k in range(TOP_K):
                row = dest_ref[(t * tm + r) * TOP_K + k]
                pltpu.make_async_copy(o_hbm.at[pl.ds(row, 1), :], buf.at[slot, pl.ds(k * tm + r, 1), :],
                                      sem.at[slot]).start(priority=k % 2)
            return carry
        lax.fori_loop(0, tm, body, 0, unroll=4)

    @pl.when(i == 0)
    def _():
        issue(0, 0)

    @pl.when(i + 1 < n)
    def _():
        issue(i + 1, (i + 1) & 1)

    slot = i & 1
    pltpu.make_async_copy(o_hbm.at[pl.ds(0, TOP_K * tm), :], buf.at[slot], sem.at[slot]).wait()
    w = w_ref[...]
    y = h_ref[...] + w[:, 0:1] * buf[slot, 0:tm, :] + w[:, 1:2] * buf[slot, tm:2 * tm, :]
    out_ref[...] = _rms(y, g_ref[...])


def _combine(o_sorted, h, wts, g, dest, *, tm):
    m, d = h.shape
    grid_spec = pltpu.PrefetchScalarGridSpec(
        num_scalar_prefetch=1,
        grid=(m // tm,),
        in_specs=[
            pl.BlockSpec(memory_space=pl.ANY),
            pl.BlockSpec((tm, d), lambda i, s: (i, 0)),
            pl.BlockSpec((tm, LANES), lambda i, s: (i, 0)),
            pl.BlockSpec((1, d), lambda i, s: (0, 0)),
        ],
        out_specs=pl.BlockSpec((tm, d), lambda i, s: (i, 0)),
        scratch_shapes=[pltpu.VMEM((2, TOP_K * tm, d), F32), pltpu.SemaphoreType.DMA((2,))],
    )
    return pl.pallas_call(
        functools.partial(_combine_kernel, tm=tm),
        grid_spec=grid_spec,
        out_shape=jax.ShapeDtypeStruct((m, d), F32),
        compiler_params=_cparams("arbitrary"),
        name="moe_combine",
    )(dest, o_sorted, h, wts, g.reshape(1, d))


def _per_sub_tile(nsub, n_sub_tiles, sub, compute, out_ref):
    for s in range(n_sub_tiles):
        rows = pl.ds(s * sub, sub)

        @pl.when(s < nsub)
        def _():
            out_ref[rows, :] = compute(rows).astype(out_ref.dtype)

        @pl.when(s >= nsub)
        def _():
            out_ref[rows, :] = jnp.zeros((sub, out_ref.shape[1]), out_ref.dtype)


def _moe_up_kernel(te_ref, first_ref, nsub_ref, nt_ref, xs_ref, wg_ref, wu_ref, h_ref, wgb, wub, *, sub):
    i = pl.program_id(1)

    @pl.when(first_ref[i] == 1)
    def _():
        wgb[...] = wg_ref[...].astype(BF16)
        wub[...] = wu_ref[...].astype(BF16)

    def compute(rows):
        x = xs_ref[rows, :]
        return _silu(_bdot(x, wgb[...])) * _bdot(x, wub[...])

    _per_sub_tile(nsub_ref[i], h_ref.shape[0] // sub, sub, compute, h_ref)


def _moe_up(xs, w_gate, w_up, plan, *, tile, sub, tf, col0, ncols):
    npad, d = xs.shape
    jb = col0 // tf
    wspec = pl.BlockSpec((None, d, tf), lambda j, i, te, fi, ns, nt: (te[i], 0, jb + j))
    grid_spec = pltpu.PrefetchScalarGridSpec(
        num_scalar_prefetch=4,
        grid=(ncols // tf, npad // tile),
        in_specs=[pl.BlockSpec((tile, d), lambda j, i, te, fi, ns, nt: (jnp.minimum(i, nt[0] - 1), 0)),
                  wspec, wspec],
        out_specs=pl.BlockSpec((tile, tf), lambda j, i, te, fi, ns, nt: (i, j)),
        scratch_shapes=[pltpu.VMEM((d, tf), BF16), pltpu.VMEM((d, tf), BF16)],
    )
    return pl.pallas_call(
        functools.partial(_moe_up_kernel, sub=sub),
        grid_spec=grid_spec,
        out_shape=jax.ShapeDtypeStruct((npad, ncols), BF16),
        compiler_params=_cparams("arbitrary", "arbitrary"),
        name="moe_up",
    )(*plan, xs, w_gate, w_up)


def _moe_down_kernel(te_ref, first_ref, nsub_ref, nt_ref, hm_ref, ht_ref, wd_ref, wdt_ref, o_ref, wdb, wdtb,
                     *, sub):
    i = pl.program_id(1)

    @pl.when(first_ref[i] == 1)
    def _():
        wdb[...] = wd_ref[...].astype(BF16)
        wdtb[...] = wdt_ref[...].astype(BF16)

    def compute(rows):
        return _bdot(hm_ref[rows, :], wdb[...]) + _bdot(ht_ref[rows, :], wdtb[...])

    _per_sub_tile(nsub_ref[i], o_ref.shape[0] // sub, sub, compute, o_ref)


def _moe_down(h_main, h_tail, w_down, plan, *, tile, sub, tn):
    npad = h_main.shape[0]
    d = w_down.shape[2]
    tail_blk = FF_MAIN // FF_TAIL
    row = lambda j, i, te, fi, ns, nt: (jnp.minimum(i, nt[0] - 1), 0)
    grid_spec = pltpu.PrefetchScalarGridSpec(
        num_scalar_prefetch=4,
        grid=(d // tn, npad // tile),
        in_specs=[
            pl.BlockSpec((tile, FF_MAIN), row),
            pl.BlockSpec((tile, FF_TAIL), row),
            pl.BlockSpec((None, FF_MAIN, tn), lambda j, i, te, fi, ns, nt: (te[i], 0, j)),
            pl.BlockSpec((None, FF_TAIL, tn), lambda j, i, te, fi, ns, nt: (te[i], tail_blk, j)),
        ],
        out_specs=pl.BlockSpec((tile, tn), lambda j, i, te, fi, ns, nt: (i, j)),
        scratch_shapes=[pltpu.VMEM((FF_MAIN, tn), BF16), pltpu.VMEM((FF_TAIL, tn), BF16)],
    )
    return pl.pallas_call(
        functools.partial(_moe_down_kernel, sub=sub),
        grid_spec=grid_spec,
        out_shape=jax.ShapeDtypeStruct((npad, d), F32),
        compiler_params=_cparams("arbitrary", "arbitrary"),
        name="moe_down",
    )(*plan, h_main, h_tail, w_down, w_down)


def _moe_plan(expert_idx, *, tile, sub, max_tiles):
    n = expert_idx.shape[0]
    flat = expert_idx.reshape(-1)
    onehot = (flat[:, None] == jnp.arange(N_EXPERTS, dtype=jnp.int32)[None, :]).astype(jnp.int32)
    incl = jnp.cumsum(onehot, axis=0)
    counts = incl[-1]
    rank = jnp.sum((incl - onehot) * onehot, axis=1)
    tiles_e = (counts + tile - 1) // tile
    tile_end = jnp.cumsum(tiles_e)
    tile_start = tile_end - tiles_e
    dest = jnp.sum(onehot * (tile_start * tile)[None, :], axis=1) + rank
    n_tiles = tile_end[-1:]
    tile_ids = jnp.arange(max_tiles, dtype=jnp.int32)
    te = jnp.sum((tile_ids[:, None] >= tile_end[None, :]).astype(jnp.int32), axis=1)
    last_e = jnp.sum((n_tiles - 1 >= tile_end).astype(jnp.int32))
    te = jnp.minimum(te, last_e)
    first = jnp.concatenate([jnp.ones((1,), jnp.int32), (te[1:] != te[:-1]).astype(jnp.int32)])
    rows_used = jnp.clip(counts[te] - (tile_ids - tile_start[te]) * tile, 0, tile)
    nsub = jnp.where(tile_ids < n_tiles, (rows_used + sub - 1) // sub, 0)
    token = jnp.arange(n * TOP_K, dtype=jnp.int32) // TOP_K
    src_token = jnp.zeros((max_tiles * tile,), jnp.int32).at[dest].set(token)
    plan = (te.astype(jnp.int32), first, nsub.astype(jnp.int32), n_tiles.astype(jnp.int32))
    return dest.astype(jnp.int32), src_token, plan


def kernel(x, meta_tokens, a_norm, a_w_in, a_conv_w, a_conv_b, a_w_rgate, a_b_rgate, a_w_igate, a_b_igate,
           a_lambda, a_w_out, kv_norm, w_kv, b_norm, b_w_q, b_lambda_q1, b_lambda_k1, b_lambda_q2, b_lambda_k2,
           b_subln, b_w_out, ffn_norm, ffn_w_gate, ffn_w_up, ffn_w_down, moe_router, moe_w_gate, moe_w_up,
           moe_w_down, final_norm):
    bsz, seq, d = x.shape
    full = N_META + seq
    tm0 = 688
    tm0_big = 2 * tm0
    tm1_big = 1024
    h = jnp.concatenate([jnp.broadcast_to(meta_tokens.astype(x.dtype)[None], (bsz, N_META, d)), x], axis=1)
    h = h.reshape(bsz * full, d)

    xg = _norm_matmul(h, a_norm[0], a_w_in[0], tm=tm0_big, tn=512, out_dtype=F32)
    y = _rglru(xg.reshape(bsz, full, 2 * d), a_conv_w[0], a_conv_b[0], a_w_rgate[0], a_b_rgate[0],
               a_w_igate[0], a_b_igate[0], a_lambda[0], c=256)
    h = _matmul_res(y.reshape(bsz * full, d), a_w_out[0], h, tm=tm0_big, tn=512)
    h = _ffn(h, ffn_norm[0], ffn_w_gate[0], ffn_w_up[0], ffn_w_down[0], tm=tm0, tf=256)

    kv = _norm_matmul(h, kv_norm, w_kv, tm=tm0_big, tn=512, out_dtype=BF16).reshape(bsz, full, -1)
    h = h.reshape(bsz, full, d)[:, N_META:].reshape(bsz * seq, d)

    lambda_init = 0.8 - 0.6 * math.exp(-0.3 * 1)
    q = _norm_matmul(h, b_norm[0], b_w_q[0], tm=tm1_big, tn=512, out_dtype=BF16, scale=HEAD_DIM ** -0.5)
    o = _diff_attention(q.reshape(bsz, seq, -1), kv, b_lambda_q1[0], b_lambda_k1[0], b_lambda_q2[0],
                        b_lambda_k2[0], b_subln[0], tq=256, lambda_init=lambda_init)
    h = _matmul_res(o.reshape(bsz * seq, -1), b_w_out[0], h, tm=tm1_big, tn=512)

    n_tok = bsz * seq
    max_tiles = n_tok * TOP_K // MOE_TILE + N_EXPERTS
    ts = dict(tile=MOE_TILE, sub=MOE_SUB)
    idx, wts = _router(h, ffn_norm[1], moe_router[0], tm=512)
    dest, src_token, plan = _moe_plan(idx[:, :TOP_K], max_tiles=max_tiles, **ts)
    xs = _gather_norm(h, ffn_norm[1], src_token, plan[2], **ts)
    hm = _moe_up(xs, moe_w_gate[0], moe_w_up[0], plan, tf=768, col0=0, ncols=FF_MAIN, **ts)
    ht = _moe_up(xs, moe_w_gate[0], moe_w_up[0], plan, tf=FF_TAIL, col0=FF_MAIN, ncols=FF_TAIL, **ts)
    o_sorted = _moe_down(hm, ht, moe_w_down[0], plan, tn=512, **ts)
    out = _combine(o_sorted, h, wts, final_norm, dest, tm=256)
    return out.reshape(bsz, seq, d)
```

```python
import functools
import math

import jax
import jax.numpy as jnp
from jax import lax
from jax.experimental import pallas as pl
from jax.experimental.pallas import tpu as pltpu

F32 = jnp.float32
BF16 = jnp.bfloat16

D_MODEL = 2048
N_META = 16
CONV_WIDTH = 4
RGLRU_C = 8.0
RNN_BLOCK = 128
HEAD_DIM = 128
N_HEADS = 8
V_HEAD_DIM = 2 * HEAD_DIM
D_FF = 5504
N_EXPERTS = 8
TOP_K = 2
EPS = 1e-6

LANES = 128
FF_TAIL = LANES
FF_MAIN = D_FF - FF_TAIL
VMEM_LIMIT = 56 * 1024 * 1024
MOE_SUB = 256
MOE_TILE = 9 * MOE_SUB


def _cparams(*sem):
    return pltpu.CompilerParams(dimension_semantics=sem, vmem_limit_bytes=VMEM_LIMIT)


def _rms(x, g):
    return x * lax.rsqrt(jnp.mean(x * x, axis=-1, keepdims=True) + EPS) * g


def _sigmoid(x):
    return 1.0 / (1.0 + jnp.exp(-x))


def _silu(x):
    return x * _sigmoid(x)


def _gelu_tanh(x):
    return 0.5 * x * (1.0 + jnp.tanh(math.sqrt(2.0 / math.pi) * (x + 0.044715 * (x * x * x))))


def _bdot(a, b):
    return jnp.dot(a, b, preferred_element_type=F32)


def _norm_matmul_kernel(x_ref, g_ref, w_ref, o_ref, xn_ref, *, scale):
    @pl.when(pl.program_id(1) == 0)
    def _():
        xn_ref[...] = _rms(x_ref[...], g_ref[...]).astype(BF16)

    acc = _bdot(xn_ref[...], w_ref[...].astype(BF16))
    if scale != 1.0:
        acc = acc * scale
    o_ref[...] = acc.astype(o_ref.dtype)


def _norm_matmul(x, g, w, *, tm, tn, out_dtype, scale=1.0):
    m, k = x.shape
    n = w.shape[1]
    return pl.pallas_call(
        functools.partial(_norm_matmul_kernel, scale=scale),
        grid=(m // tm, n // tn),
        in_specs=[
            pl.BlockSpec((tm, k), lambda i, j: (i, 0)),
            pl.BlockSpec((1, k), lambda i, j: (0, 0)),
            pl.BlockSpec((k, tn), lambda i, j: (0, j)),
        ],
        out_specs=pl.BlockSpec((tm, tn), lambda i, j: (i, j)),
        out_shape=jax.ShapeDtypeStruct((m, n), out_dtype),
        scratch_shapes=[pltpu.VMEM((tm, k), BF16)],
        compiler_params=_cparams("parallel", "arbitrary"),
        name="norm_matmul",
    )(x, g.reshape(1, k), w)


def _matmul_res_kernel(y_ref, w_ref, r_ref, o_ref):
    o_ref[...] = r_ref[...] + _bdot(y_ref[...], w_ref[...].astype(BF16))


def _matmul_res(y, w, res, *, tm, tn):
    m, k = y.shape
    n = w.shape[1]
    return pl.pallas_call(
        _matmul_res_kernel,
        grid=(m // tm, n // tn),
        in_specs=[
            pl.BlockSpec((tm, k), lambda i, j: (i, 0)),
            pl.BlockSpec((k, tn), lambda i, j: (0, j)),
            pl.BlockSpec((tm, tn), lambda i, j: (i, j)),
        ],
        out_specs=pl.BlockSpec((tm, tn), lambda i, j: (i, j)),
        out_shape=jax.ShapeDtypeStruct((m, n), F32),
        compiler_params=_cparams("parallel", "arbitrary"),
        name="matmul_res",
    )(y, w, res)


def _rglru_kernel(x_ref, g_ref, cw_ref, cb_ref, wr_ref, br_ref, wi_ref, bi_ref, lam_ref, y_ref, a_s, b_s):
    seq, c = a_s.shape
    x = x_ref[0]
    row = lax.broadcasted_iota(jnp.int32, (seq, c), 0)
    xc = jnp.broadcast_to(cb_ref[...], (seq, c))
    for k in range(CONV_WIDTH):
        shift = CONV_WIDTH - 1 - k
        xs = x if shift == 0 else jnp.where(row >= shift, pltpu.roll(x, shift, 0), 0.0)
        xc = xc + xs * cw_ref[k:k + 1, :]

    lam = lam_ref[...]
    softplus_neg_lam = jnp.maximum(-lam, 0.0) + jnp.log1p(jnp.exp(-jnp.abs(lam)))
    for gi in range(c // RNN_BLOCK):
        sl = slice(gi * RNN_BLOCK, (gi + 1) * RNN_BLOCK)
        xcg = xc[:, sl]
        xb = xcg.astype(BF16)
        r = _sigmoid(_bdot(xb, wr_ref[gi].astype(BF16)) + br_ref[:, sl])
        ig = _sigmoid(_bdot(xb, wi_ref[gi].astype(BF16)) + bi_ref[:, sl])
        log_a = (-RGLRU_C) * r * softplus_neg_lam[:, sl]
        a = jnp.exp(log_a)
        a_s[:, sl] = a
        b_s[:, sl] = jnp.sqrt(jnp.tanh(-log_a) * (1.0 + a * a)) * (ig * xcg)

    sub = lax.broadcasted_iota(jnp.int32, (8, c), 0)

    def tile(i, h_prev):
        r0 = pl.multiple_of(i * 8, 8)
        a = a_s[pl.ds(r0, 8), :]
        b = b_s[pl.ds(r0, 8), :]
        for s in (1, 2, 4):
            keep = sub >= s
            a_sh = jnp.where(keep, pltpu.roll(a, s, 0), 1.0)
            b_sh = jnp.where(keep, pltpu.roll(b, s, 0), 0.0)
            b = a * b_sh + b
            a = a * a_sh
        h = a * h_prev + b
        b_s[pl.ds(r0, 8), :] = h
        return jnp.broadcast_to(h[7:8, :], (8, c))

    lax.fori_loop(0, seq // 8, tile, jnp.zeros((8, c), F32))
    y_ref[0] = (b_s[...] * _gelu_tanh(g_ref[0])).astype(BF16)


def _rglru(xg, conv_w, conv_b, w_rg, b_rg, w_ig, b_ig, lam, *, c):
    bsz, seq, two_d = xg.shape
    d = two_d // 2
    nc = d // c
    gpb = c // RNN_BLOCK
    vec = lambda v: v.reshape(1, d)
    vspec = pl.BlockSpec((1, c), lambda b, j: (0, j))
    return pl.pallas_call(
        _rglru_kernel,
        grid=(bsz, nc),
        in_specs=[
            pl.BlockSpec((1, seq, c), lambda b, j: (b, 0, j)),
            pl.BlockSpec((1, seq, c), lambda b, j: (b, 0, nc + j)),
            pl.BlockSpec((CONV_WIDTH, c), lambda b, j: (0, j)),
            vspec,
            pl.BlockSpec((gpb, RNN_BLOCK, RNN_BLOCK), lambda b, j: (j, 0, 0)),
            vspec,
            pl.BlockSpec((gpb, RNN_BLOCK, RNN_BLOCK), lambda b, j: (j, 0, 0)),
            vspec,
            vspec,
        ],
        out_specs=pl.BlockSpec((1, seq, c), lambda b, j: (b, 0, j)),
        out_shape=jax.ShapeDtypeStruct((bsz, seq, d), BF16),
        scratch_shapes=[pltpu.VMEM((seq, c), F32), pltpu.VMEM((seq, c), F32)],
        compiler_params=_cparams("parallel", "parallel"),
        name="rglru",
    )(xg, xg, conv_w, vec(conv_b), w_rg, vec(b_rg), w_ig, vec(b_ig), vec(lam))


def _ffn_kernel(x_ref, g_ref, wg_ref, wu_ref, wd_ref, wgt_ref, wut_ref, wdt_ref, o_ref, xn_ref):
    j = pl.program_id(1)

    def contrib(wg, wu, wd):
        xn = xn_ref[...]
        h = _silu(_bdot(xn, wg[...].astype(BF16))) * _bdot(xn, wu[...].astype(BF16))
        return _bdot(h.astype(BF16), wd[...].astype(BF16))

    @pl.when(j == 0)
    def _():
        x = x_ref[...]
        xn_ref[...] = _rms(x, g_ref[...]).astype(BF16)
        o_ref[...] = x + contrib(wgt_ref, wut_ref, wdt_ref)

    o_ref[...] += contrib(wg_ref, wu_ref, wd_ref)


def _ffn(x, g, w_gate, w_up, w_down, *, tm, tf):
    m, d = x.shape
    tail_blk = FF_MAIN // FF_TAIL
    return pl.pallas_call(
        _ffn_kernel,
        grid=(m // tm, FF_MAIN // tf),
        in_specs=[
            pl.BlockSpec((tm, d), lambda i, j: (i, 0), pipeline_mode=pl.Buffered(1)),
            pl.BlockSpec((1, d), lambda i, j: (0, 0)),
            pl.BlockSpec((d, tf), lambda i, j: (0, j)),
            pl.BlockSpec((d, tf), lambda i, j: (0, j)),
            pl.BlockSpec((tf, d), lambda i, j: (j, 0)),
            pl.BlockSpec((d, FF_TAIL), lambda i, j: (0, tail_blk)),
            pl.BlockSpec((d, FF_TAIL), lambda i, j: (0, tail_blk)),
            pl.BlockSpec((FF_TAIL, d), lambda i, j: (tail_blk, 0)),
        ],
        out_specs=pl.BlockSpec((tm, d), lambda i, j: (i, 0), pipeline_mode=pl.Buffered(1)),
        out_shape=jax.ShapeDtypeStruct((m, d), F32),
        scratch_shapes=[pltpu.VMEM((tm, d), BF16)],
        compiler_params=_cparams("parallel", "arbitrary"),
        name="dense_ffn",
    )(x, g.reshape(1, d), w_gate, w_up, w_down, w_gate, w_up, w_down)


def _attn_kernel(slopes_ref, q1_ref, q2_ref, k1_ref, k2_ref, v_ref, lq1_ref, lk1_ref, lq2_ref, lk2_ref,
                 sg_ref, o_ref, *, tq, lambda_init):
    slope = slopes_ref[pl.program_id(1)]
    seq = q1_ref.shape[1]
    q_refs = (q1_ref, q2_ref)
    k_refs = (k1_ref, k2_ref)
    nt = (((1,), (1,)), ((), ()))
    lam = (jnp.exp(jnp.sum(lq1_ref[...] * lk1_ref[...], axis=-1, keepdims=True))
           - jnp.exp(jnp.sum(lq2_ref[...] * lk2_ref[...], axis=-1, keepdims=True)) + lambda_init)
    lane = lax.broadcasted_iota(jnp.int32, (1, LANES), 1)
    meta_mask = lane < N_META
    col = lax.broadcasted_iota(jnp.int32, (1, tq), 1)
    diag_mask = col <= lax.broadcasted_iota(jnp.int32, (tq, tq), 0)
    diag_bias = slope * col.astype(F32)

    for qi in range(seq // tq):
        r0 = qi * tq
        pieces = [(0, LANES, slope * (lane - (N_META + r0)).astype(F32), meta_mask),
                  (N_META + r0, tq, diag_bias, diag_mask)]
        if qi > 0:
            off_col = lax.broadcasted_iota(jnp.int32, (1, r0), 1)
            pieces.append((N_META, r0, slope * (off_col - r0).astype(F32), None))

        exps, scales = [], []
        for s in range(2):
            q = q_refs[s][0, r0:r0 + tq, :]
            scores = []
            for k0, rows, bias, mask in pieces:
                sc = lax.dot_general(q, k_refs[s][0, k0:k0 + rows, :], nt, preferred_element_type=F32) + bias
                scores.append(sc if mask is None else jnp.where(mask, sc, -jnp.inf))
            m = functools.reduce(jnp.maximum, [jnp.max(sc, axis=-1, keepdims=True) for sc in scores])
            e = [jnp.exp(sc - m) for sc in scores]
            l = functools.reduce(jnp.add, [jnp.sum(x, axis=-1, keepdims=True) for x in e])
            exps.append(e)
            scales.append(1.0 / l if s == 0 else lam / l)

        o = None
        for (k0, rows, _, _), e1, e2 in zip(pieces, exps[0], exps[1]):
            p = (e1 * scales[0] - e2 * scales[1]).astype(BF16)
            pv = _bdot(p, v_ref[0, k0:k0 + rows, :])
            o = pv if o is None else o + pv
        o_ref[0, r0:r0 + tq, :] = (_rms(o, sg_ref[...]) * (1.0 - lambda_init)).astype(o_ref.dtype)


def _diff_attention(q, kv, lq1, lk1, lq2, lk2, subln_g, *, tq, lambda_init):
    bsz, seq, _ = q.shape
    full = kv.shape[1]
    slopes = jnp.asarray([2.0 ** (-8.0 * (i + 1) / N_HEADS) for i in range(N_HEADS)], F32)
    vec = lambda v: v.reshape(1, -1)
    lspec = pl.BlockSpec((1, HEAD_DIM), lambda b, h, s: (0, 0))
    grid_spec = pltpu.PrefetchScalarGridSpec(
        num_scalar_prefetch=1,
        grid=(bsz, N_HEADS),
        in_specs=[
            pl.BlockSpec((1, seq, HEAD_DIM), lambda b, h, s: (b, 0, h)),
            pl.BlockSpec((1, seq, HEAD_DIM), lambda b, h, s: (b, 0, N_HEADS + h)),
            pl.BlockSpec((1, full, HEAD_DIM), lambda b, h, s: (b, 0, h)),
            pl.BlockSpec((1, full, HEAD_DIM), lambda b, h, s: (b, 0, N_HEADS + h)),
            pl.BlockSpec((1, full, V_HEAD_DIM), lambda b, h, s: (b, 0, N_HEADS + h)),
            lspec, lspec, lspec, lspec,
            pl.BlockSpec((1, V_HEAD_DIM), lambda b, h, s: (0, 0)),
        ],
        out_specs=pl.BlockSpec((1, seq, V_HEAD_DIM), lambda b, h, s: (b, 0, h)),
    )
    return pl.pallas_call(
        functools.partial(_attn_kernel, tq=tq, lambda_init=lambda_init),
        grid_spec=grid_spec,
        out_shape=jax.ShapeDtypeStruct((bsz, seq, N_HEADS * V_HEAD_DIM), BF16),
        compiler_params=_cparams("parallel", "parallel"),
        name="diff_attention",
    )(slopes, q, q, kv, kv, kv, vec(lq1), vec(lk1), vec(lq2), vec(lk2), vec(subln_g))


def _router_kernel(x_ref, g_ref, r_ref, idx_ref, w_ref):
    u = _rms(x_ref[...], g_ref[...])
    logits = jnp.dot(u, r_ref[...], precision=lax.Precision.HIGHEST, preferred_element_type=F32)
    lane = lax.broadcasted_iota(jnp.int32, logits.shape, 1).astype(F32)
    lg = jnp.where(lane < N_EXPERTS, logits, -jnp.inf)
    m1 = jnp.max(lg, axis=-1, keepdims=True)
    i1 = jnp.min(jnp.where(lg == m1, lane, float(LANES)), axis=-1, keepdims=True)
    lg2 = jnp.where(lane == i1, -jnp.inf, lg)
    m2 = jnp.max(lg2, axis=-1, keepdims=True)
    i2 = jnp.min(jnp.where(lg2 == m2, lane, float(LANES)), axis=-1, keepdims=True)
    e2 = jnp.exp(m2 - m1)
    w1 = 1.0 / (1.0 + e2)
    w2 = e2 / (1.0 + e2)
    idx_ref[...] = jnp.where(lane == 0.0, i1, jnp.where(lane == 1.0, i2, 0.0)).astype(jnp.int32)
    w_ref[...] = jnp.where(lane == 0.0, w1, jnp.where(lane == 1.0, w2, 0.0))


def _router(x, g, router, *, tm):
    m, d = x.shape
    router_p = jnp.pad(router, ((0, 0), (0, LANES - N_EXPERTS)))
    out = pl.BlockSpec((tm, LANES), lambda i: (i, 0))
    return pl.pallas_call(
        _router_kernel,
        grid=(m // tm,),
        in_specs=[
            pl.BlockSpec((tm, d), lambda i: (i, 0)),
            pl.BlockSpec((1, d), lambda i: (0, 0)),
            pl.BlockSpec((d, LANES), lambda i: (0, 0)),
        ],
        out_specs=[out, out],
        out_shape=[jax.ShapeDtypeStruct((m, LANES), jnp.int32), jax.ShapeDtypeStruct((m, LANES), F32)],
        compiler_params=_cparams("parallel"),
        name="moe_router",
    )(x, g.reshape(1, d), router_p)


def _gather_norm_kernel(src_ref, used_ref, h_hbm, g_ref, xs_ref, buf, sem, *, sub):
    k = pl.program_id(0)

    def issue(kk, slot):
        base = kk * sub

        def body(r8, carry):
            for u in range(8):
                r = r8 * 8 + u
                tok = src_ref[base + r]
                pltpu.make_async_copy(h_hbm.at[pl.ds(tok, 1), :], buf.at[slot, pl.ds(r, 1), :],
                                      sem.at[slot]).start(priority=u % 2)
            return carry
        lax.fori_loop(0, sub // 8, body, 0)

    @pl.when(jnp.logical_and(k == 0, used_ref[0] == 1))
    def _():
        issue(0, 0)

    @pl.when(k + 1 < pl.num_programs(0))
    def _():
        @pl.when(used_ref[k + 1] == 1)
        def _():
            issue(k + 1, (k + 1) & 1)

    @pl.when(used_ref[k] == 1)
    def _():
        slot = k & 1
        pltpu.make_async_copy(h_hbm.at[pl.ds(0, sub), :], buf.at[slot], sem.at[slot]).wait()
        xs_ref[...] = _rms(buf[slot], g_ref[...]).astype(BF16)

    @pl.when(used_ref[k] == 0)
    def _():
        xs_ref[...] = jnp.zeros_like(xs_ref)


def _gather_norm(h, g, src_token, used, *, sub):
    _, d = h.shape
    grid_spec = pltpu.PrefetchScalarGridSpec(
        num_scalar_prefetch=2,
        grid=(used.shape[0],),
        in_specs=[pl.BlockSpec(memory_space=pl.ANY), pl.BlockSpec((1, d), lambda k, s, u: (0, 0))],
        out_specs=pl.BlockSpec((sub, d), lambda k, s, u: (k, 0)),
        scratch_shapes=[pltpu.VMEM((2, sub, d), F32), pltpu.SemaphoreType.DMA((2,))],
    )
    return pl.pallas_call(
        functools.partial(_gather_norm_kernel, sub=sub),
        grid_spec=grid_spec,
        out_shape=jax.ShapeDtypeStruct((src_token.shape[0], d), BF16),
        compiler_params=_cparams("arbitrary"),
        name="moe_gather",
    )(src_token, used, h, g.reshape(1, d))


def _combine_kernel(dest_ref, o_hbm, h_ref, w_ref, g_ref, out_ref, buf, sem, *, tm):
    i = pl.program_id(0)
    n = pl.num_programs(0)

    def issue(t, slot):
        def body(r, carry):
            for k in range(TOP_K):
                row = dest_ref[(t * tm + r) * TOP_K + k]
                pltpu.make_async_copy(o_hbm.at[pl.ds(row, 1), :], buf.at[slot, pl.ds(k * tm + r, 1), :],
                                      sem.at[slot]).start(priority=k % 2)
            return carry
        lax.fori_loop(0, tm, body, 0, unroll=4)

    @pl.when(i == 0)
    def _():
        issue(0, 0)

    @pl.when(i + 1 < n)
    def _():
        issue(i + 1, (i + 1) & 1)

    slot = i & 1
    pltpu.make_async_copy(o_hbm.at[pl.ds(0, TOP_K * tm), :], buf.at[slot], sem.at[slot]).wait()
    w = w_ref[...]
    y = h_ref[...] + w[:, 0:1] * buf[slot, 0:tm, :] + w[:, 1:2] * buf[slot, tm:2 * tm, :]
    out_ref[...] = _rms(y, g_ref[...])


def _combine(o_sorted, h, wts, g, dest, *, tm):
    m, d = h.shape
    grid_spec = pltpu.PrefetchScalarGridSpec(
        num_scalar_prefetch=1,
        grid=(m // tm,),
        in_specs=[
            pl.BlockSpec(memory_space=pl.ANY),
            pl.BlockSpec((tm, d), lambda i, s: (i, 0)),
            pl.BlockSpec((tm, LANES), lambda i, s: (i, 0)),
            pl.BlockSpec((1, d), lambda i, s: (0, 0)),
        ],
        out_specs=pl.BlockSpec((tm, d), lambda i, s: (i, 0)),
        scratch_shapes=[pltpu.VMEM((2, TOP_K * tm, d), F32), pltpu.SemaphoreType.DMA((2,))],
    )
    return pl.pallas_call(
        functools.partial(_combine_kernel, tm=tm),
        grid_spec=grid_spec,
        out_shape=jax.ShapeDtypeStruct((m, d), F32),
        compiler_params=_cparams("arbitrary"),
        name="moe_combine",
    )(dest, o_sorted, h, wts, g.reshape(1, d))


def _moe_ffn_kernel(te_ref, nsub_ref, nt_ref, x_ref, wg_ref, wu_ref, wd_ref, wgt_ref, wut_ref, wdt_ref, o_ref,
                    wgb, wub, wdb, wgtb, wutb, wdtb, *, sub):
    t = pl.program_id(0)
    j = pl.program_id(1)

    def accumulate(wg, wu, wd):
        def rows_at(start, size):
            rows = pl.ds(pl.multiple_of(start, sub), size)
            x = x_ref[rows, :]
            h = _silu(_bdot(x, wg[...])) * _bdot(x, wu[...])
            o_ref[rows, :] += _bdot(h.astype(BF16), wd[...])

        def pair(p, carry):
            rows_at(p * (2 * sub), 2 * sub)
            return carry

        ns = nsub_ref[t]
        lax.fori_loop(0, lax.shift_right_logical(ns, 1), pair, 0)

        @pl.when((ns & 1) == 1)
        def _():
            rows_at((ns - 1) * sub, sub)

    @pl.when(j == 0)
    def _():
        o_ref[...] = jnp.zeros_like(o_ref)

    @pl.when(t < nt_ref[0])
    def _():
        @pl.when(j == 0)
        def _():
            wgtb[...] = wgt_ref[...].astype(BF16)
            wutb[...] = wut_ref[...].astype(BF16)
            wdtb[...] = wdt_ref[...].astype(BF16)
            accumulate(wgtb, wutb, wdtb)

        wgb[...] = wg_ref[...].astype(BF16)
        wub[...] = wu_ref[...].astype(BF16)
        wdb[...] = wd_ref[...].astype(BF16)
        accumulate(wgb, wub, wdb)


def _moe_ffn(xs, w_gate, w_up, w_down, plan, *, tile, sub, tf):
    npad, d = xs.shape
    nj = FF_MAIN // tf
    tail_blk = FF_MAIN // FF_TAIL

    def live(t, nt):
        return jnp.minimum(t, nt[0] - 1)

    def jj(t, j, nt):
        return jnp.where(t < nt[0], j, nj - 1)

    row = lambda t, j, te, ns, nt: (live(t, nt), 0)
    resident = dict(pipeline_mode=pl.Buffered(1))
    grid_spec = pltpu.PrefetchScalarGridSpec(
        num_scalar_prefetch=3,
        grid=(npad // tile, nj),
        in_specs=[
            pl.BlockSpec((tile, d), row, **resident),
            pl.BlockSpec((None, d, tf), lambda t, j, te, ns, nt: (te[t], 0, jj(t, j, nt))),
            pl.BlockSpec((None, d, tf), lambda t, j, te, ns, nt: (te[t], 0, jj(t, j, nt))),
            pl.BlockSpec((None, tf, d), lambda t, j, te, ns, nt: (te[t], jj(t, j, nt), 0)),
            pl.BlockSpec((None, d, FF_TAIL), lambda t, j, te, ns, nt: (te[t], 0, tail_blk)),
            pl.BlockSpec((None, d, FF_TAIL), lambda t, j, te, ns, nt: (te[t], 0, tail_blk)),
            pl.BlockSpec((None, FF_TAIL, d), lambda t, j, te, ns, nt: (te[t], tail_blk, 0)),
        ],
        out_specs=pl.BlockSpec((tile, d), lambda t, j, te, ns, nt: (t, 0), **resident),
        scratch_shapes=[pltpu.VMEM((d, tf), BF16), pltpu.VMEM((d, tf), BF16), pltpu.VMEM((tf, d), BF16),
                        pltpu.VMEM((d, FF_TAIL), BF16), pltpu.VMEM((d, FF_TAIL), BF16),
                        pltpu.VMEM((FF_TAIL, d), BF16)],
    )
    return pl.pallas_call(
        functools.partial(_moe_ffn_kernel, sub=sub),
        grid_spec=grid_spec,
        out_shape=jax.ShapeDtypeStruct((npad, d), F32),
        compiler_params=_cparams("arbitrary", "arbitrary"),
        name="moe_ffn",
    )(*plan, xs, w_gate, w_up, w_down, w_gate, w_up, w_down)


def _moe_plan(expert_idx, *, tile, sub, max_tiles):
    n = expert_idx.shape[0]
    flat = expert_idx.reshape(-1)
    onehot = (flat[:, None] == jnp.arange(N_EXPERTS, dtype=jnp.int32)[None, :]).astype(jnp.int32)
    incl = jnp.cumsum(onehot, axis=0)
    counts = incl[-1]
    rank = jnp.sum((incl - onehot) * onehot, axis=1)
    tiles_e = (counts + tile - 1) // tile
    tile_end = jnp.cumsum(tiles_e)
    tile_start = tile_end - tiles_e
    dest = jnp.sum(onehot * (tile_start * tile)[None, :], axis=1) + rank
    n_tiles = tile_end[-1:]
    tile_ids = jnp.arange(max_tiles, dtype=jnp.int32)
    te = jnp.sum((tile_ids[:, None] >= tile_end[None, :]).astype(jnp.int32), axis=1)
    last_e = jnp.sum((n_tiles - 1 >= tile_end).astype(jnp.int32))
    te = jnp.minimum(te, last_e)
    rows_used = jnp.clip(counts[te] - (tile_ids - tile_start[te]) * tile, 0, tile)
    nsub = jnp.where(tile_ids < n_tiles, (rows_used + sub - 1) // sub, 0).astype(jnp.int32)
    token = jnp.arange(n * TOP_K, dtype=jnp.int32) // TOP_K
    src_token = jnp.zeros((max_tiles * tile,), jnp.int32).at[dest].set(token)
    used = (jnp.arange(tile // sub, dtype=jnp.int32)[None, :] < nsub[:, None]).reshape(-1).astype(jnp.int32)
    plan = (te.astype(jnp.int32), nsub, n_tiles.astype(jnp.int32))
    return dest.astype(jnp.int32), src_token, used, plan


def kernel(x, meta_tokens, a_norm, a_w_in, a_conv_w, a_conv_b, a_w_rgate, a_b_rgate, a_w_igate, a_b_igate,
           a_lambda, a_w_out, kv_norm, w_kv, b_norm, b_w_q, b_lambda_q1, b_lambda_k1, b_lambda_q2, b_lambda_k2,
           b_subln, b_w_out, ffn_norm, ffn_w_gate, ffn_w_up, ffn_w_down, moe_router, moe_w_gate, moe_w_up,
           moe_w_down, final_norm):
    bsz, seq, d = x.shape
    full = N_META + seq
    tm0 = 688
    tm0_big = 2 * tm0
    tm1_big = 1024
    h = jnp.concatenate([jnp.broadcast_to(meta_tokens.astype(x.dtype)[None], (bsz, N_META, d)), x], axis=1)
    h = h.reshape(bsz * full, d)

    xg = _norm_matmul(h, a_norm[0], a_w_in[0], tm=tm0_big, tn=512, out_dtype=F32)
    y = _rglru(xg.reshape(bsz, full, 2 * d), a_conv_w[0], a_conv_b[0], a_w_rgate[0], a_b_rgate[0],
               a_w_igate[0], a_b_igate[0], a_lambda[0], c=256)
    h = _matmul_res(y.reshape(bsz * full, d), a_w_out[0], h, tm=tm0_big, tn=512)
    h = _ffn(h, ffn_norm[0], ffn_w_gate[0], ffn_w_up[0], ffn_w_down[0], tm=tm0_big, tf=256)

    kv = _norm_matmul(h, kv_norm, w_kv, tm=tm0_big, tn=512, out_dtype=BF16).reshape(bsz, full, -1)
    h = h.reshape(bsz, full, d)[:, N_META:].reshape(bsz * seq, d)

    lambda_init = 0.8 - 0.6 * math.exp(-0.3 * 1)
    q = _norm_matmul(h, b_norm[0], b_w_q[0], tm=tm1_big, tn=512, out_dtype=BF16, scale=HEAD_DIM ** -0.5)
    o = _diff_attention(q.reshape(bsz, seq, -1), kv, b_lambda_q1[0], b_lambda_k1[0], b_lambda_q2[0],
                        b_lambda_k2[0], b_subln[0], tq=256, lambda_init=lambda_init)
    h = _matmul_res(o.reshape(bsz * seq, -1), b_w_out[0], h, tm=tm1_big, tn=512)

    n_tok = bsz * seq
    max_tiles = n_tok * TOP_K // MOE_TILE + N_EXPERTS
    idx, wts = _router(h, ffn_norm[1], moe_router[0], tm=512)
    dest, src_token, used, plan = _moe_plan(idx[:, :TOP_K], tile=MOE_TILE, sub=MOE_SUB, max_tiles=max_tiles)
    xs = _gather_norm(h, ffn_norm[1], src_token, used, sub=MOE_SUB)
    o_sorted = _moe_ffn(xs, moe_w_gate[0], moe_w_up[0], moe_w_down[0], plan, tile=MOE_TILE, sub=MOE_SUB, tf=256)
    out = _combine(o_sorted, h, wts, final_norm, dest, tm=256)
    return out.reshape(bsz, seq, d)
```

```python
import functools
import math

import jax
import jax.numpy as jnp
from jax import lax
from jax.experimental import pallas as pl
from jax.experimental.pallas import tpu as pltpu

F32 = jnp.float32
BF16 = jnp.bfloat16

D_MODEL = 2048
N_META = 16
CONV_WIDTH = 4
RGLRU_C = 8.0
RNN_BLOCK = 128
HEAD_DIM = 128
N_HEADS = 8
V_HEAD_DIM = 2 * HEAD_DIM
D_FF = 5504
N_EXPERTS = 8
TOP_K = 2
EPS = 1e-6

LANES = 128
FF_TAIL = LANES
FF_MAIN = D_FF - FF_TAIL
VMEM_LIMIT = 58 * 1024 * 1024
MOE_SUB = 256
MOE_TILE = 9 * MOE_SUB


def _cparams(*sem):
    return pltpu.CompilerParams(dimension_semantics=sem, vmem_limit_bytes=VMEM_LIMIT)


def _rms(x, g):
    return x * lax.rsqrt(jnp.mean(x * x, axis=-1, keepdims=True) + EPS) * g


def _sigmoid(x):
    return 0.5 * jnp.tanh(0.5 * x) + 0.5


def _silu(x):
    return x * _sigmoid(x)


def _gelu_tanh(x):
    c = math.sqrt(2.0 / math.pi)
    return x * (0.5 * jnp.tanh(x * (c + (c * 0.044715) * (x * x))) + 0.5)


def _bdot(a, b):
    return jnp.dot(a, b, preferred_element_type=F32)


def _norm_matmul_kernel(x_ref, g_ref, w_ref, o_ref, xn_ref, *, scale):
    @pl.when(pl.program_id(1) == 0)
    def _():
        xn_ref[...] = _rms(x_ref[...], g_ref[...]).astype(BF16)

    acc = _bdot(xn_ref[...], w_ref[...].astype(BF16))
    if scale != 1.0:
        acc = acc * scale
    o_ref[...] = acc.astype(o_ref.dtype)


def _norm_matmul(x, g, w, *, tm, tn, out_dtype, scale=1.0):
    m, k = x.shape
    n = w.shape[1]
    return pl.pallas_call(
        functools.partial(_norm_matmul_kernel, scale=scale),
        grid=(m // tm, n // tn),
        in_specs=[
            pl.BlockSpec((tm, k), lambda i, j: (i, 0)),
            pl.BlockSpec((1, k), lambda i, j: (0, 0)),
            pl.BlockSpec((k, tn), lambda i, j: (0, j)),
        ],
        out_specs=pl.BlockSpec((tm, tn), lambda i, j: (i, j)),
        out_shape=jax.ShapeDtypeStruct((m, n), out_dtype),
        scratch_shapes=[pltpu.VMEM((tm, k), BF16)],
        compiler_params=_cparams("parallel", "arbitrary"),
        name="norm_matmul",
    )(x, g.reshape(1, k), w)


def _matmul_res_kernel(y_ref, w_ref, r_ref, o_ref):
    o_ref[...] = r_ref[...] + _bdot(y_ref[...], w_ref[...].astype(BF16))


def _matmul_res(y, w, res, *, tm, tn):
    m, k = y.shape
    n = w.shape[1]
    return pl.pallas_call(
        _matmul_res_kernel,
        grid=(m // tm, n // tn),
        in_specs=[
            pl.BlockSpec((tm, k), lambda i, j: (i, 0)),
            pl.BlockSpec((k, tn), lambda i, j: (0, j)),
            pl.BlockSpec((tm, tn), lambda i, j: (i, j)),
        ],
        out_specs=pl.BlockSpec((tm, tn), lambda i, j: (i, j)),
        out_shape=jax.ShapeDtypeStruct((m, n), F32),
        compiler_params=_cparams("parallel", "arbitrary"),
        name="matmul_res",
    )(y, w, res)


def _rglru_kernel(x_ref, g_ref, cw_ref, cb_ref, wr_ref, br_ref, wi_ref, bi_ref, lam_ref, y_ref, a_s, b_s):
    seq, c = a_s.shape
    x = x_ref[0]
    row = lax.broadcasted_iota(jnp.int32, (seq, c), 0)
    xc = jnp.broadcast_to(cb_ref[...], (seq, c))
    for k in range(CONV_WIDTH):
        shift = CONV_WIDTH - 1 - k
        xs = x if shift == 0 else jnp.where(row >= shift, pltpu.roll(x, shift, 0), 0.0)
        xc = xc + xs * cw_ref[k:k + 1, :]

    lam = lam_ref[...]
    softplus_neg_lam = jnp.maximum(-lam, 0.0) + jnp.log1p(jnp.exp(-jnp.abs(lam)))
    for gi in range(c // RNN_BLOCK):
        sl = slice(gi * RNN_BLOCK, (gi + 1) * RNN_BLOCK)
        xcg = xc[:, sl]
        xb = xcg.astype(BF16)
        r = _sigmoid(_bdot(xb, wr_ref[gi].astype(BF16)) + br_ref[:, sl])
        ig = _sigmoid(_bdot(xb, wi_ref[gi].astype(BF16)) + bi_ref[:, sl])
        log_a = (-RGLRU_C) * r * softplus_neg_lam[:, sl]
        a = jnp.exp(log_a)
        a_s[:, sl] = a
        b_s[:, sl] = jnp.sqrt(jnp.tanh(-log_a) * (1.0 + a * a)) * (ig * xcg)

    sub = lax.broadcasted_iota(jnp.int32, (8, c), 0)
    tiles_per_trip = 6
    rows_per_trip = 8 * tiles_per_trip
    assert seq % rows_per_trip == 0

    def trip(i, h_prev):
        r0 = pl.multiple_of(i * rows_per_trip, 8)
        scanned = []
        for u in range(tiles_per_trip):
            a = a_s[pl.ds(r0 + 8 * u, 8), :]
            b = b_s[pl.ds(r0 + 8 * u, 8), :]
            for s in (1, 2, 4):
                keep = sub >= s
                a_sh = jnp.where(keep, pltpu.roll(a, s, 0), 1.0)
                b_sh = jnp.where(keep, pltpu.roll(b, s, 0), 0.0)
                b = a * b_sh + b
                a = a * a_sh
            scanned.append((a, b))
        for u, (a, b) in enumerate(scanned):
            h = a * h_prev + b
            b_s[pl.ds(r0 + 8 * u, 8), :] = h
            h_prev = jnp.broadcast_to(h[7:8, :], (8, c))
        return h_prev

    lax.fori_loop(0, seq // rows_per_trip, trip, jnp.zeros((8, c), F32))
    y_ref[0] = (b_s[...] * _gelu_tanh(g_ref[0])).astype(BF16)


def _rglru(xg, conv_w, conv_b, w_rg, b_rg, w_ig, b_ig, lam, *, c):
    bsz, seq, two_d = xg.shape
    d = two_d // 2
    nc = d // c
    gpb = c // RNN_BLOCK
    vec = lambda v: v.reshape(1, d)
    vspec = pl.BlockSpec((1, c), lambda b, j: (0, j))
    return pl.pallas_call(
        _rglru_kernel,
        grid=(bsz, nc),
        in_specs=[
            pl.BlockSpec((1, seq, c), lambda b, j: (b, 0, j)),
            pl.BlockSpec((1, seq, c), lambda b, j: (b, 0, nc + j)),
            pl.BlockSpec((CONV_WIDTH, c), lambda b, j: (0, j)),
            vspec,
            pl.BlockSpec((gpb, RNN_BLOCK, RNN_BLOCK), lambda b, j: (j, 0, 0)),
            vspec,
            pl.BlockSpec((gpb, RNN_BLOCK, RNN_BLOCK), lambda b, j: (j, 0, 0)),
            vspec,
            vspec,
        ],
        out_specs=pl.BlockSpec((1, seq, c), lambda b, j: (b, 0, j)),
        out_shape=jax.ShapeDtypeStruct((bsz, seq, d), BF16),
        scratch_shapes=[pltpu.VMEM((seq, c), F32), pltpu.VMEM((seq, c), F32)],
        compiler_params=_cparams("parallel", "parallel"),
        name="rglru",
    )(xg, xg, conv_w, vec(conv_b), w_rg, vec(b_rg), w_ig, vec(b_ig), vec(lam))


def _ffn_kernel(x_ref, g_ref, wg_ref, wu_ref, wd_ref, wgt_ref, wut_ref, wdt_ref, o_ref, xn_ref):
    j = pl.program_id(1)

    def contrib(wg, wu, wd):
        xn = xn_ref[...]
        h = _silu(_bdot(xn, wg[...].astype(BF16))) * _bdot(xn, wu[...].astype(BF16))
        return _bdot(h.astype(BF16), wd[...].astype(BF16))

    @pl.when(j == 0)
    def _():
        x = x_ref[...]
        xn_ref[...] = _rms(x, g_ref[...]).astype(BF16)
        o_ref[...] = x + contrib(wgt_ref, wut_ref, wdt_ref)

    o_ref[...] += contrib(wg_ref, wu_ref, wd_ref)


def _ffn(x, g, w_gate, w_up, w_down, *, tm, tf):
    m, d = x.shape
    tail_blk = FF_MAIN // FF_TAIL
    return pl.pallas_call(
        _ffn_kernel,
        grid=(m // tm, FF_MAIN // tf),
        in_specs=[
            pl.BlockSpec((tm, d), lambda i, j: (i, 0), pipeline_mode=pl.Buffered(1)),
            pl.BlockSpec((1, d), lambda i, j: (0, 0)),
            pl.BlockSpec((d, tf), lambda i, j: (0, j)),
            pl.BlockSpec((d, tf), lambda i, j: (0, j)),
            pl.BlockSpec((tf, d), lambda i, j: (j, 0)),
            pl.BlockSpec((d, FF_TAIL), lambda i, j: (0, tail_blk)),
            pl.BlockSpec((d, FF_TAIL), lambda i, j: (0, tail_blk)),
            pl.BlockSpec((FF_TAIL, d), lambda i, j: (tail_blk, 0)),
        ],
        out_specs=pl.BlockSpec((tm, d), lambda i, j: (i, 0), pipeline_mode=pl.Buffered(1)),
        out_shape=jax.ShapeDtypeStruct((m, d), F32),
        scratch_shapes=[pltpu.VMEM((tm, d), BF16)],
        compiler_params=_cparams("parallel", "arbitrary"),
        name="dense_ffn",
    )(x, g.reshape(1, d), w_gate, w_up, w_down, w_gate, w_up, w_down)


def _attn_kernel(slopes_ref, q1_ref, q2_ref, k1_ref, k2_ref, v_ref, lq1_ref, lk1_ref, lq2_ref, lk2_ref,
                 sg_ref, o_ref, *, tq, lambda_init):
    slope = slopes_ref[pl.program_id(1)]
    seq = q1_ref.shape[1]
    q_refs = (q1_ref, q2_ref)
    k_refs = (k1_ref, k2_ref)
    nt = (((1,), (1,)), ((), ()))
    lam = (jnp.exp(jnp.sum(lq1_ref[...] * lk1_ref[...], axis=-1, keepdims=True))
           - jnp.exp(jnp.sum(lq2_ref[...] * lk2_ref[...], axis=-1, keepdims=True)) + lambda_init)
    lane = lax.broadcasted_iota(jnp.int32, (1, LANES), 1)
    meta_mask = lane < N_META
    col = lax.broadcasted_iota(jnp.int32, (1, tq), 1)
    diag_mask = col <= lax.broadcasted_iota(jnp.int32, (tq, tq), 0)
    diag_bias = slope * col.astype(F32)

    for qi in range(seq // tq):
        r0 = qi * tq
        pieces = [(0, LANES, slope * (lane - (N_META + r0)).astype(F32), meta_mask),
                  (N_META + r0, tq, diag_bias, diag_mask)]
        if qi > 0:
            off_col = lax.broadcasted_iota(jnp.int32, (1, r0), 1)
            pieces.append((N_META, r0, slope * (off_col - r0).astype(F32), None))

        exps, scales = [], []
        for s in range(2):
            q = q_refs[s][0, r0:r0 + tq, :]
            scores = []
            for k0, rows, bias, mask in pieces:
                sc = lax.dot_general(q, k_refs[s][0, k0:k0 + rows, :], nt, preferred_element_type=F32) + bias
                scores.append(sc if mask is None else jnp.where(mask, sc, -jnp.inf))
            m = functools.reduce(jnp.maximum, [jnp.max(sc, axis=-1, keepdims=True) for sc in scores])
            e = [jnp.exp(sc - m) for sc in scores]
            l = functools.reduce(jnp.add, [jnp.sum(x, axis=-1, keepdims=True) for x in e])
            exps.append(e)
            scales.append(1.0 / l if s == 0 else lam / l)

        o = None
        for (k0, rows, _, _), e1, e2 in zip(pieces, exps[0], exps[1]):
            p = (e1 * scales[0] - e2 * scales[1]).astype(BF16)
            pv = _bdot(p, v_ref[0, k0:k0 + rows, :])
            o = pv if o is None else o + pv
        o_ref[0, r0:r0 + tq, :] = (_rms(o, sg_ref[...]) * (1.0 - lambda_init)).astype(o_ref.dtype)


def _diff_attention(q, kv, lq1, lk1, lq2, lk2, subln_g, *, tq, lambda_init):
    bsz, seq, _ = q.shape
    full = kv.shape[1]
    slopes = jnp.asarray([2.0 ** (-8.0 * (i + 1) / N_HEADS) for i in range(N_HEADS)], F32)
    vec = lambda v: v.reshape(1, -1)
    lspec = pl.BlockSpec((1, HEAD_DIM), lambda b, h, s: (0, 0))
    grid_spec = pltpu.PrefetchScalarGridSpec(
        num_scalar_prefetch=1,
        grid=(bsz, N_HEADS),
        in_specs=[
            pl.BlockSpec((1, seq, HEAD_DIM), lambda b, h, s: (b, 0, h)),
            pl.BlockSpec((1, seq, HEAD_DIM), lambda b, h, s: (b, 0, N_HEADS + h)),
            pl.BlockSpec((1, full, HEAD_DIM), lambda b, h, s: (b, 0, h)),
            pl.BlockSpec((1, full, HEAD_DIM), lambda b, h, s: (b, 0, N_HEADS + h)),
            pl.BlockSpec((1, full, V_HEAD_DIM), lambda b, h, s: (b, 0, N_HEADS + h)),
            lspec, lspec, lspec, lspec,
            pl.BlockSpec((1, V_HEAD_DIM), lambda b, h, s: (0, 0)),
        ],
        out_specs=pl.BlockSpec((1, seq, V_HEAD_DIM), lambda b, h, s: (b, 0, h)),
    )
    return pl.pallas_call(
        functools.partial(_attn_kernel, tq=tq, lambda_init=lambda_init),
        grid_spec=grid_spec,
        out_shape=jax.ShapeDtypeStruct((bsz, seq, N_HEADS * V_HEAD_DIM), BF16),
        compiler_params=_cparams("parallel", "parallel"),
        name="diff_attention",
    )(slopes, q, q, kv, kv, kv, vec(lq1), vec(lk1), vec(lq2), vec(lk2), vec(subln_g))


def _router_kernel(x_ref, g_ref, r_ref, idx_ref, w_ref):
    u = _rms(x_ref[...], g_ref[...])
    logits = jnp.dot(u, r_ref[...], precision=lax.Precision.HIGHEST, preferred_element_type=F32)
    lane = lax.broadcasted_iota(jnp.int32, logits.shape, 1).astype(F32)
    lg = jnp.where(lane < N_EXPERTS, logits, -jnp.inf)
    m1 = jnp.max(lg, axis=-1, keepdims=True)
    i1 = jnp.min(jnp.where(lg == m1, lane, float(LANES)), axis=-1, keepdims=True)
    lg2 = jnp.where(lane == i1, -jnp.inf, lg)
    m2 = jnp.max(lg2, axis=-1, keepdims=True)
    i2 = jnp.min(jnp.where(lg2 == m2, lane, float(LANES)), axis=-1, keepdims=True)
    e2 = jnp.exp(m2 - m1)
    w1 = 1.0 / (1.0 + e2)
    w2 = e2 / (1.0 + e2)
    idx_ref[...] = jnp.where(lane == 0.0, i1, jnp.where(lane == 1.0, i2, 0.0)).astype(jnp.int32)
    w_ref[...] = jnp.where(lane == 0.0, w1, jnp.where(lane == 1.0, w2, 0.0))


def _router(x, g, router, *, tm):
    m, d = x.shape
    router_p = jnp.pad(router, ((0, 0), (0, LANES - N_EXPERTS)))
    out = pl.BlockSpec((tm, LANES), lambda i: (i, 0))
    return pl.pallas_call(
        _router_kernel,
        grid=(m // tm,),
        in_specs=[
            pl.BlockSpec((tm, d), lambda i: (i, 0)),
            pl.BlockSpec((1, d), lambda i: (0, 0)),
            pl.BlockSpec((d, LANES), lambda i: (0, 0)),
        ],
        out_specs=[out, out],
        out_shape=[jax.ShapeDtypeStruct((m, LANES), jnp.int32), jax.ShapeDtypeStruct((m, LANES), F32)],
        compiler_params=_cparams("parallel"),
        name="moe_router",
    )(x, g.reshape(1, d), router_p)


def _gather_norm_kernel(src_ref, used_ref, h_hbm, g_ref, xs_ref, buf, sem, *, sub):
    k = pl.program_id(0)

    def issue(kk, slot):
        base = kk * sub

        def body(r8, carry):
            for u in range(8):
                r = r8 * 8 + u
                tok = src_ref[base + r]
                pltpu.make_async_copy(h_hbm.at[pl.ds(tok, 1), :], buf.at[slot, pl.ds(r, 1), :],
                                      sem.at[slot]).start(priority=u % 2)
            return carry
        lax.fori_loop(0, sub // 8, body, 0)

    @pl.when(jnp.logical_and(k == 0, used_ref[0] == 1))
    def _():
        issue(0, 0)

    @pl.when(k + 1 < pl.num_programs(0))
    def _():
        @pl.when(used_ref[k + 1] == 1)
        def _():
            issue(k + 1, (k + 1) & 1)

    @pl.when(used_ref[k] == 1)
    def _():
        slot = k & 1
        pltpu.make_async_copy(h_hbm.at[pl.ds(0, sub), :], buf.at[slot], sem.at[slot]).wait()
        xs_ref[...] = _rms(buf[slot], g_ref[...]).astype(BF16)

    @pl.when(used_ref[k] == 0)
    def _():
        xs_ref[...] = jnp.zeros_like(xs_ref)


def _gather_norm(h, g, src_token, used, *, sub):
    _, d = h.shape
    grid_spec = pltpu.PrefetchScalarGridSpec(
        num_scalar_prefetch=2,
        grid=(used.shape[0],),
        in_specs=[pl.BlockSpec(memory_space=pl.ANY), pl.BlockSpec((1, d), lambda k, s, u: (0, 0))],
        out_specs=pl.BlockSpec((sub, d), lambda k, s, u: (k, 0)),
        scratch_shapes=[pltpu.VMEM((2, sub, d), F32), pltpu.SemaphoreType.DMA((2,))],
    )
    return pl.pallas_call(
        functools.partial(_gather_norm_kernel, sub=sub),
        grid_spec=grid_spec,
        out_shape=jax.ShapeDtypeStruct((src_token.shape[0], d), BF16),
        compiler_params=_cparams("arbitrary"),
        name="moe_gather",
    )(src_token, used, h, g.reshape(1, d))


def _combine_kernel(dest_ref, o_hbm, h_ref, w_ref, g_ref, out_ref, buf, sem, *, tm):
    i = pl.program_id(0)
    n = pl.num_programs(0)

    def issue(t, slot):
        def body(r, carry):
            for k in range(TOP_K):
                row = dest_ref[(t * tm + r) * TOP_K + k]
                pltpu.make_async_copy(o_hbm.at[pl.ds(row, 1), :], buf.at[slot, pl.ds(k * tm + r, 1), :],
                                      sem.at[slot]).start(priority=k % 2)
            return carry
        lax.fori_loop(0, tm, body, 0, unroll=4)

    @pl.when(i == 0)
    def _():
        issue(0, 0)

    @pl.when(i + 1 < n)
    def _():
        issue(i + 1, (i + 1) & 1)

    slot = i & 1
    pltpu.make_async_copy(o_hbm.at[pl.ds(0, TOP_K * tm), :], buf.at[slot], sem.at[slot]).wait()
    w = w_ref[...]
    y = h_ref[...] + w[:, 0:1] * buf[slot, 0:tm, :] + w[:, 1:2] * buf[slot, tm:2 * tm, :]
    out_ref[...] = _rms(y, g_ref[...])


def _combine(o_sorted, h, wts, g, dest, *, tm):
    m, d = h.shape
    grid_spec = pltpu.PrefetchScalarGridSpec(
        num_scalar_prefetch=1,
        grid=(m // tm,),
        in_specs=[
            pl.BlockSpec(memory_space=pl.ANY),
            pl.BlockSpec((tm, d), lambda i, s: (i, 0)),
            pl.BlockSpec((tm, LANES), lambda i, s: (i, 0)),
            pl.BlockSpec((1, d), lambda i, s: (0, 0)),
        ],
        out_specs=pl.BlockSpec((tm, d), lambda i, s: (i, 0)),
        scratch_shapes=[pltpu.VMEM((2, TOP_K * tm, d), F32), pltpu.SemaphoreType.DMA((2,))],
    )
    return pl.pallas_call(
        functools.partial(_combine_kernel, tm=tm),
        grid_spec=grid_spec,
        out_shape=jax.ShapeDtypeStruct((m, d), F32),
        compiler_params=_cparams("arbitrary"),
        name="moe_combine",
    )(dest, o_sorted, h, wts, g.reshape(1, d))


def _moe_ffn_kernel(te_ref, nsub_ref, nt_ref, x_ref, wg_ref, wu_ref, wd_ref, wgt_ref, wut_ref, wdt_ref, o_ref,
                    wgb, wub, wdb, wgtb, wutb, wdtb, *, sub):
    t = pl.program_id(0)
    j = pl.program_id(1)

    def accumulate(wg, wu, wd):
        def rows_at(start, size):
            rows = pl.ds(pl.multiple_of(start, sub), size)
            x = x_ref[rows, :]
            h = _silu(_bdot(x, wg[...])) * _bdot(x, wu[...])
            o_ref[rows, :] += _bdot(h.astype(BF16), wd[...])

        def pair(p, carry):
            rows_at(p * (2 * sub), 2 * sub)
            return carry

        ns = nsub_ref[t]
        lax.fori_loop(0, lax.shift_right_logical(ns, 1), pair, 0)

        @pl.when((ns & 1) == 1)
        def _():
            rows_at((ns - 1) * sub, sub)

    @pl.when(j == 0)
    def _():
        o_ref[...] = jnp.zeros_like(o_ref)

    @pl.when(t < nt_ref[0])
    def _():
        @pl.when(j == 0)
        def _():
            wgtb[...] = wgt_ref[...].astype(BF16)
            wutb[...] = wut_ref[...].astype(BF16)
            wdtb[...] = wdt_ref[...].astype(BF16)
            accumulate(wgtb, wutb, wdtb)

        wgb[...] = wg_ref[...].astype(BF16)
        wub[...] = wu_ref[...].astype(BF16)
        wdb[...] = wd_ref[...].astype(BF16)
        accumulate(wgb, wub, wdb)


def _moe_ffn(xs, w_gate, w_up, w_down, plan, *, tile, sub, tf):
    npad, d = xs.shape
    nj = FF_MAIN // tf
    tail_blk = FF_MAIN // FF_TAIL

    def live(t, nt):
        return jnp.minimum(t, nt[0] - 1)

    def jj(t, j, nt):
        return jnp.where(t < nt[0], j, nj - 1)

    row = lambda t, j, te, ns, nt: (live(t, nt), 0)
    resident = dict(pipeline_mode=pl.Buffered(1))
    grid_spec = pltpu.PrefetchScalarGridSpec(
        num_scalar_prefetch=3,
        grid=(npad // tile, nj),
        in_specs=[
            pl.BlockSpec((tile, d), row),
            pl.BlockSpec((None, d, tf), lambda t, j, te, ns, nt: (te[t], 0, jj(t, j, nt))),
            pl.BlockSpec((None, d, tf), lambda t, j, te, ns, nt: (te[t], 0, jj(t, j, nt))),
            pl.BlockSpec((None, tf, d), lambda t, j, te, ns, nt: (te[t], jj(t, j, nt), 0)),
            pl.BlockSpec((None, d, FF_TAIL), lambda t, j, te, ns, nt: (te[t], 0, tail_blk), **resident),
            pl.BlockSpec((None, d, FF_TAIL), lambda t, j, te, ns, nt: (te[t], 0, tail_blk), **resident),
            pl.BlockSpec((None, FF_TAIL, d), lambda t, j, te, ns, nt: (te[t], tail_blk, 0), **resident),
        ],
        out_specs=pl.BlockSpec((tile, d), lambda t, j, te, ns, nt: (t, 0), **resident),
        scratch_shapes=[pltpu.VMEM((d, tf), BF16), pltpu.VMEM((d, tf), BF16), pltpu.VMEM((tf, d), BF16),
                        pltpu.VMEM((d, FF_TAIL), BF16), pltpu.VMEM((d, FF_TAIL), BF16),
                        pltpu.VMEM((FF_TAIL, d), BF16)],
    )
    return pl.pallas_call(
        functools.partial(_moe_ffn_kernel, sub=sub),
        grid_spec=grid_spec,
        out_shape=jax.ShapeDtypeStruct((npad, d), F32),
        compiler_params=_cparams("arbitrary", "arbitrary"),
        name="moe_ffn",
    )(*plan, xs, w_gate, w_up, w_down, w_gate, w_up, w_down)


def _moe_plan(expert_idx, *, tile, sub, max_tiles):
    n = expert_idx.shape[0]
    flat = expert_idx.reshape(-1)
    onehot = (flat[:, None] == jnp.arange(N_EXPERTS, dtype=jnp.int32)[None, :]).astype(jnp.int32)
    incl = jnp.cumsum(onehot, axis=0)
    counts = incl[-1]
    rank = jnp.sum((incl - onehot) * onehot, axis=1)
    tiles_e = (counts + tile - 1) // tile
    tile_end = jnp.cumsum(tiles_e)
    tile_start = tile_end - tiles_e
    dest = jnp.sum(onehot * (tile_start * tile)[None, :], axis=1) + rank
    n_tiles = tile_end[-1:]
    tile_ids = jnp.arange(max_tiles, dtype=jnp.int32)
    te = jnp.sum((tile_ids[:, None] >= tile_end[None, :]).astype(jnp.int32), axis=1)
    last_e = jnp.sum((n_tiles - 1 >= tile_end).astype(jnp.int32))
    te = jnp.minimum(te, last_e)
    rows_used = jnp.clip(counts[te] - (tile_ids - tile_start[te]) * tile, 0, tile)
    nsub = jnp.where(tile_ids < n_tiles, (rows_used + sub - 1) // sub, 0).astype(jnp.int32)
    token = jnp.arange(n * TOP_K, dtype=jnp.int32) // TOP_K
    src_token = jnp.zeros((max_tiles * tile,), jnp.int32).at[dest].set(token)
    used = (jnp.arange(tile // sub, dtype=jnp.int32)[None, :] < nsub[:, None]).reshape(-1).astype(jnp.int32)
    plan = (te.astype(jnp.int32), nsub, n_tiles.astype(jnp.int32))
    return dest.astype(jnp.int32), src_token, used, plan


def kernel(x, meta_tokens, a_norm, a_w_in, a_conv_w, a_conv_b, a_w_rgate, a_b_rgate, a_w_igate, a_b_igate,
           a_lambda, a_w_out, kv_norm, w_kv, b_norm, b_w_q, b_lambda_q1, b_lambda_k1, b_lambda_q2, b_lambda_k2,
           b_subln, b_w_out, ffn_norm, ffn_w_gate, ffn_w_up, ffn_w_down, moe_router, moe_w_gate, moe_w_up,
           moe_w_down, final_norm):
    bsz, seq, d = x.shape
    full = N_META + seq
    tm0 = 688
    tm0_big = 2 * tm0
    tm1_big = 1024
    h = jnp.concatenate([jnp.broadcast_to(meta_tokens.astype(x.dtype)[None], (bsz, N_META, d)), x], axis=1)
    h = h.reshape(bsz * full, d)

    xg = _norm_matmul(h, a_norm[0], a_w_in[0], tm=tm0_big, tn=512, out_dtype=F32)
    y = _rglru(xg.reshape(bsz, full, 2 * d), a_conv_w[0], a_conv_b[0], a_w_rgate[0], a_b_rgate[0],
               a_w_igate[0], a_b_igate[0], a_lambda[0], c=256)
    h = _matmul_res(y.reshape(bsz * full, d), a_w_out[0], h, tm=tm0_big, tn=512)
    h = _ffn(h, ffn_norm[0], ffn_w_gate[0], ffn_w_up[0], ffn_w_down[0], tm=tm0_big, tf=256)

    kv = _norm_matmul(h, kv_norm, w_kv, tm=tm0_big, tn=512, out_dtype=BF16).reshape(bsz, full, -1)
    h = h.reshape(bsz, full, d)[:, N_META:].reshape(bsz * seq, d)

    lambda_init = 0.8 - 0.6 * math.exp(-0.3 * 1)
    q = _norm_matmul(h, b_norm[0], b_w_q[0], tm=tm1_big, tn=512, out_dtype=BF16, scale=HEAD_DIM ** -0.5)
    o = _diff_attention(q.reshape(bsz, seq, -1), kv, b_lambda_q1[0], b_lambda_k1[0], b_lambda_q2[0],
                        b_lambda_k2[0], b_subln[0], tq=256, lambda_init=lambda_init)
    h = _matmul_res(o.reshape(bsz * seq, -1), b_w_out[0], h, tm=tm1_big, tn=512)

    n_tok = bsz * seq
    max_tiles = n_tok * TOP_K // MOE_TILE + N_EXPERTS
    idx, wts = _router(h, ffn_norm[1], moe_router[0], tm=512)
    dest, src_token, used, plan = _moe_plan(idx[:, :TOP_K], tile=MOE_TILE, sub=MOE_SUB, max_tiles=max_tiles)
    xs = _gather_norm(h, ffn_norm[1], src_token, used, sub=MOE_SUB)
    o_sorted = _moe_ffn(xs, moe_w_gate[0], moe_w_up[0], moe_w_down[0], plan, tile=MOE_TILE, sub=MOE_SUB, tf=256)
    out = _combine(o_sorted, h, wts, final_norm, dest, tm=256)
    return out.reshape(bsz, seq, d)
```

```python
import functools
import math

import jax
import jax.numpy as jnp
from jax import lax
from jax.experimental import pallas as pl
from jax.experimental.pallas import tpu as pltpu

F32 = jnp.float32
BF16 = jnp.bfloat16

D_MODEL = 2048
N_META = 16
CONV_WIDTH = 4
RGLRU_C = 8.0
RNN_BLOCK = 128
HEAD_DIM = 128
N_HEADS = 8
V_HEAD_DIM = 2 * HEAD_DIM
D_FF = 5504
N_EXPERTS = 8
TOP_K = 2
EPS = 1e-6

LANES = 128
FF_TAIL = LANES
FF_MAIN = D_FF - FF_TAIL
VMEM_LIMIT = 58 * 1024 * 1024
MOE_SUB = 256
MOE_TILE = 9 * MOE_SUB


def _cparams(*sem):
    return pltpu.CompilerParams(dimension_semantics=sem, vmem_limit_bytes=VMEM_LIMIT)


def _rms(x, g):
    return x * lax.rsqrt(jnp.mean(x * x, axis=-1, keepdims=True) + EPS) * g


def _sigmoid(x):
    return 0.5 * jnp.tanh(0.5 * x) + 0.5


def _silu(x):
    return x * _sigmoid(x)


def _gelu_tanh(x):
    c = math.sqrt(2.0 / math.pi)
    return x * (0.5 * jnp.tanh(x * (c + (c * 0.044715) * (x * x))) + 0.5)


def _bdot(a, b):
    return jnp.dot(a, b, preferred_element_type=F32)


def _norm_matmul_kernel(x_ref, g_ref, w_ref, o_ref, xn_ref, *, scale):
    @pl.when(pl.program_id(1) == 0)
    def _():
        xn_ref[...] = _rms(x_ref[...], g_ref[...]).astype(BF16)

    acc = _bdot(xn_ref[...], w_ref[...].astype(BF16))
    if scale != 1.0:
        acc = acc * scale
    o_ref[...] = acc.astype(o_ref.dtype)


def _norm_matmul(x, g, w, *, tm, tn, out_dtype, scale=1.0):
    m, k = x.shape
    n = w.shape[1]
    return pl.pallas_call(
        functools.partial(_norm_matmul_kernel, scale=scale),
        grid=(m // tm, n // tn),
        in_specs=[
            pl.BlockSpec((tm, k), lambda i, j: (i, 0)),
            pl.BlockSpec((1, k), lambda i, j: (0, 0)),
            pl.BlockSpec((k, tn), lambda i, j: (0, j)),
        ],
        out_specs=pl.BlockSpec((tm, tn), lambda i, j: (i, j)),
        out_shape=jax.ShapeDtypeStruct((m, n), out_dtype),
        scratch_shapes=[pltpu.VMEM((tm, k), BF16)],
        compiler_params=_cparams("parallel", "arbitrary"),
        name="norm_matmul",
    )(x, g.reshape(1, k), w)


def _matmul_res_kernel(y_ref, w_ref, r_ref, o_ref):
    o_ref[...] = r_ref[...] + _bdot(y_ref[...], w_ref[...].astype(BF16))


def _matmul_res(y, w, res, *, tm, tn):
    m, k = y.shape
    n = w.shape[1]
    return pl.pallas_call(
        _matmul_res_kernel,
        grid=(m // tm, n // tn),
        in_specs=[
            pl.BlockSpec((tm, k), lambda i, j: (i, 0)),
            pl.BlockSpec((k, tn), lambda i, j: (0, j)),
            pl.BlockSpec((tm, tn), lambda i, j: (i, j)),
        ],
        out_specs=pl.BlockSpec((tm, tn), lambda i, j: (i, j)),
        out_shape=jax.ShapeDtypeStruct((m, n), F32),
        compiler_params=_cparams("parallel", "arbitrary"),
        name="matmul_res",
    )(y, w, res)


def _rglru_kernel(x_ref, g_ref, cw_ref, cb_ref, wr_ref, br_ref, wi_ref, bi_ref, lam_ref, y_ref, a_s, b_s):
    seq, c = a_s.shape
    x = x_ref[0]
    row = lax.broadcasted_iota(jnp.int32, (seq, c), 0)
    xc = jnp.broadcast_to(cb_ref[...], (seq, c))
    for k in range(CONV_WIDTH):
        shift = CONV_WIDTH - 1 - k
        xs = x if shift == 0 else jnp.where(row >= shift, pltpu.roll(x, shift, 0), 0.0)
        xc = xc + xs * cw_ref[k:k + 1, :]

    lam = lam_ref[...]
    softplus_neg_lam = jnp.maximum(-lam, 0.0) + jnp.log1p(jnp.exp(-jnp.abs(lam)))
    for gi in range(c // RNN_BLOCK):
        sl = slice(gi * RNN_BLOCK, (gi + 1) * RNN_BLOCK)
        xcg = xc[:, sl]
        xb = xcg.astype(BF16)
        r = _sigmoid(_bdot(xb, wr_ref[gi].astype(BF16)) + br_ref[:, sl])
        ig = _sigmoid(_bdot(xb, wi_ref[gi].astype(BF16)) + bi_ref[:, sl])
        log_a = (-RGLRU_C) * r * softplus_neg_lam[:, sl]
        a = jnp.exp(log_a)
        a_s[:, sl] = a
        b_s[:, sl] = jnp.sqrt(jnp.tanh(-log_a) * (1.0 + a * a)) * (ig * xcg)

    sub = lax.broadcasted_iota(jnp.int32, (8, c), 0)
    tiles_per_trip = 6
    rows_per_trip = 8 * tiles_per_trip
    assert seq % rows_per_trip == 0

    def trip(i, h_prev):
        r0 = pl.multiple_of(i * rows_per_trip, 8)
        scanned = []
        for u in range(tiles_per_trip):
            a = a_s[pl.ds(r0 + 8 * u, 8), :]
            b = b_s[pl.ds(r0 + 8 * u, 8), :]
            for s in (1, 2, 4):
                keep = sub >= s
                a_sh = jnp.where(keep, pltpu.roll(a, s, 0), 1.0)
                b_sh = jnp.where(keep, pltpu.roll(b, s, 0), 0.0)
                b = a * b_sh + b
                a = a * a_sh
            scanned.append((a, b))
        for u, (a, b) in enumerate(scanned):
            h = a * h_prev + b
            b_s[pl.ds(r0 + 8 * u, 8), :] = h
            h_prev = jnp.broadcast_to(h[7:8, :], (8, c))
        return h_prev

    lax.fori_loop(0, seq // rows_per_trip, trip, jnp.zeros((8, c), F32))
    y_ref[0] = (b_s[...] * _gelu_tanh(g_ref[0])).astype(BF16)


def _rglru(xg, conv_w, conv_b, w_rg, b_rg, w_ig, b_ig, lam, *, c):
    bsz, seq, two_d = xg.shape
    d = two_d // 2
    nc = d // c
    gpb = c // RNN_BLOCK
    vec = lambda v: v.reshape(1, d)
    vspec = pl.BlockSpec((1, c), lambda b, j: (0, j))
    return pl.pallas_call(
        _rglru_kernel,
        grid=(bsz, nc),
        in_specs=[
            pl.BlockSpec((1, seq, c), lambda b, j: (b, 0, j)),
            pl.BlockSpec((1, seq, c), lambda b, j: (b, 0, nc + j)),
            pl.BlockSpec((CONV_WIDTH, c), lambda b, j: (0, j)),
            vspec,
            pl.BlockSpec((gpb, RNN_BLOCK, RNN_BLOCK), lambda b, j: (j, 0, 0)),
            vspec,
            pl.BlockSpec((gpb, RNN_BLOCK, RNN_BLOCK), lambda b, j: (j, 0, 0)),
            vspec,
            vspec,
        ],
        out_specs=pl.BlockSpec((1, seq, c), lambda b, j: (b, 0, j)),
        out_shape=jax.ShapeDtypeStruct((bsz, seq, d), BF16),
        scratch_shapes=[pltpu.VMEM((seq, c), F32), pltpu.VMEM((seq, c), F32)],
        compiler_params=_cparams("parallel", "parallel"),
        name="rglru",
    )(xg, xg, conv_w, vec(conv_b), w_rg, vec(b_rg), w_ig, vec(b_ig), vec(lam))


def _ffn_kernel(x_ref, g_ref, wg_ref, wu_ref, wd_ref, wgt_ref, wut_ref, wdt_ref, o_ref, xn_ref):
    j = pl.program_id(1)

    def contrib(wg, wu, wd):
        xn = xn_ref[...]
        h = _silu(_bdot(xn, wg[...].astype(BF16))) * _bdot(xn, wu[...].astype(BF16))
        return _bdot(h.astype(BF16), wd[...].astype(BF16))

    @pl.when(j == 0)
    def _():
        x = x_ref[...]
        xn_ref[...] = _rms(x, g_ref[...]).astype(BF16)
        o_ref[...] = x + contrib(wgt_ref, wut_ref, wdt_ref)

    o_ref[...] += contrib(wg_ref, wu_ref, wd_ref)


def _ffn(x, g, w_gate, w_up, w_down, *, tm, tf):
    m, d = x.shape
    tail_blk = FF_MAIN // FF_TAIL
    return pl.pallas_call(
        _ffn_kernel,
        grid=(m // tm, FF_MAIN // tf),
        in_specs=[
            pl.BlockSpec((tm, d), lambda i, j: (i, 0), pipeline_mode=pl.Buffered(1)),
            pl.BlockSpec((1, d), lambda i, j: (0, 0)),
            pl.BlockSpec((d, tf), lambda i, j: (0, j)),
            pl.BlockSpec((d, tf), lambda i, j: (0, j)),
            pl.BlockSpec((tf, d), lambda i, j: (j, 0)),
            pl.BlockSpec((d, FF_TAIL), lambda i, j: (0, tail_blk)),
            pl.BlockSpec((d, FF_TAIL), lambda i, j: (0, tail_blk)),
            pl.BlockSpec((FF_TAIL, d), lambda i, j: (tail_blk, 0)),
        ],
        out_specs=pl.BlockSpec((tm, d), lambda i, j: (i, 0), pipeline_mode=pl.Buffered(1)),
        out_shape=jax.ShapeDtypeStruct((m, d), F32),
        scratch_shapes=[pltpu.VMEM((tm, d), BF16)],
        compiler_params=_cparams("parallel", "arbitrary"),
        name="dense_ffn",
    )(x, g.reshape(1, d), w_gate, w_up, w_down, w_gate, w_up, w_down)


def _attn_kernel(slopes_ref, q1_ref, q2_ref, k1_ref, k2_ref, v_ref, lq1_ref, lk1_ref, lq2_ref, lk2_ref,
                 sg_ref, o_ref, *, tq, lambda_init):
    slope = slopes_ref[pl.program_id(1)]
    seq = q1_ref.shape[1]
    q_refs = (q1_ref, q2_ref)
    k_refs = (k1_ref, k2_ref)
    nt = (((1,), (1,)), ((), ()))
    lam = (jnp.exp(jnp.sum(lq1_ref[...] * lk1_ref[...], axis=-1, keepdims=True))
           - jnp.exp(jnp.sum(lq2_ref[...] * lk2_ref[...], axis=-1, keepdims=True)) + lambda_init)
    lane = lax.broadcasted_iota(jnp.int32, (1, LANES), 1)
    meta_mask = lane < N_META
    col = lax.broadcasted_iota(jnp.int32, (1, tq), 1)
    diag_mask = col <= lax.broadcasted_iota(jnp.int32, (tq, tq), 0)
    diag_bias = slope * col.astype(F32)

    for qi in range(seq // tq):
        r0 = qi * tq
        pieces = [(0, LANES, slope * (lane - (N_META + r0)).astype(F32), meta_mask),
                  (N_META + r0, tq, diag_bias, diag_mask)]
        if qi > 0:
            off_col = lax.broadcasted_iota(jnp.int32, (1, r0), 1)
            pieces.append((N_META, r0, slope * (off_col - r0).astype(F32), None))

        exps, scales = [], []
        for s in range(2):
            q = q_refs[s][0, r0:r0 + tq, :]
            scores = []
            for k0, rows, bias, mask in pieces:
                sc = lax.dot_general(q, k_refs[s][0, k0:k0 + rows, :], nt, preferred_element_type=F32) + bias
                scores.append(sc if mask is None else jnp.where(mask, sc, -jnp.inf))
            m = functools.reduce(jnp.maximum, [jnp.max(sc, axis=-1, keepdims=True) for sc in scores])
            e = [jnp.exp(sc - m) for sc in scores]
            l = functools.reduce(jnp.add, [jnp.sum(x, axis=-1, keepdims=True) for x in e])
            exps.append(e)
            scales.append(1.0 / l if s == 0 else lam / l)

        o = None
        for (k0, rows, _, _), e1, e2 in zip(pieces, exps[0], exps[1]):
            p = (e1 * scales[0] - e2 * scales[1]).astype(BF16)
            pv = _bdot(p, v_ref[0, k0:k0 + rows, :])
            o = pv if o is None else o + pv
        o_ref[0, r0:r0 + tq, :] = (_rms(o, sg_ref[...]) * (1.0 - lambda_init)).astype(o_ref.dtype)


def _diff_attention(q, kv, lq1, lk1, lq2, lk2, subln_g, *, tq, lambda_init):
    bsz, seq, _ = q.shape
    full = kv.shape[1]
    slopes = jnp.asarray([2.0 ** (-8.0 * (i + 1) / N_HEADS) for i in range(N_HEADS)], F32)
    vec = lambda v: v.reshape(1, -1)
    lspec = pl.BlockSpec((1, HEAD_DIM), lambda b, h, s: (0, 0))
    grid_spec = pltpu.PrefetchScalarGridSpec(
        num_scalar_prefetch=1,
        grid=(bsz, N_HEADS),
        in_specs=[
            pl.BlockSpec((1, seq, HEAD_DIM), lambda b, h, s: (b, 0, h)),
            pl.BlockSpec((1, seq, HEAD_DIM), lambda b, h, s: (b, 0, N_HEADS + h)),
            pl.BlockSpec((1, full, HEAD_DIM), lambda b, h, s: (b, 0, h)),
            pl.BlockSpec((1, full, HEAD_DIM), lambda b, h, s: (b, 0, N_HEADS + h)),
            pl.BlockSpec((1, full, V_HEAD_DIM), lambda b, h, s: (b, 0, N_HEADS + h)),
            lspec, lspec, lspec, lspec,
            pl.BlockSpec((1, V_HEAD_DIM), lambda b, h, s: (0, 0)),
        ],
        out_specs=pl.BlockSpec((1, seq, V_HEAD_DIM), lambda b, h, s: (b, 0, h)),
    )
    return pl.pallas_call(
        functools.partial(_attn_kernel, tq=tq, lambda_init=lambda_init),
        grid_spec=grid_spec,
        out_shape=jax.ShapeDtypeStruct((bsz, seq, N_HEADS * V_HEAD_DIM), BF16),
        compiler_params=_cparams("parallel", "parallel"),
        name="diff_attention",
    )(slopes, q, q, kv, kv, kv, vec(lq1), vec(lk1), vec(lq2), vec(lk2), vec(subln_g))


def _router_kernel(x_ref, g_ref, r_ref, idx_ref, w_ref):
    u = _rms(x_ref[...], g_ref[...])
    logits = jnp.dot(u, r_ref[...], precision=lax.Precision.HIGHEST, preferred_element_type=F32)
    lane = lax.broadcasted_iota(jnp.int32, logits.shape, 1).astype(F32)
    lg = jnp.where(lane < N_EXPERTS, logits, -jnp.inf)
    m1 = jnp.max(lg, axis=-1, keepdims=True)
    i1 = jnp.min(jnp.where(lg == m1, lane, float(LANES)), axis=-1, keepdims=True)
    lg2 = jnp.where(lane == i1, -jnp.inf, lg)
    m2 = jnp.max(lg2, axis=-1, keepdims=True)
    i2 = jnp.min(jnp.where(lg2 == m2, lane, float(LANES)), axis=-1, keepdims=True)
    e2 = jnp.exp(m2 - m1)
    w1 = 1.0 / (1.0 + e2)
    w2 = e2 / (1.0 + e2)
    idx_ref[...] = jnp.where(lane == 0.0, i1, jnp.where(lane == 1.0, i2, 0.0)).astype(jnp.int32)
    w_ref[...] = jnp.where(lane == 0.0, w1, jnp.where(lane == 1.0, w2, 0.0))


def _router(x, g, router, *, tm):
    m, d = x.shape
    router_p = jnp.pad(router, ((0, 0), (0, LANES - N_EXPERTS)))
    out = pl.BlockSpec((tm, LANES), lambda i: (i, 0))
    return pl.pallas_call(
        _router_kernel,
        grid=(m // tm,),
        in_specs=[
            pl.BlockSpec((tm, d), lambda i: (i, 0)),
            pl.BlockSpec((1, d), lambda i: (0, 0)),
            pl.BlockSpec((d, LANES), lambda i: (0, 0)),
        ],
        out_specs=[out, out],
        out_shape=[jax.ShapeDtypeStruct((m, LANES), jnp.int32), jax.ShapeDtypeStruct((m, LANES), F32)],
        compiler_params=_cparams("parallel"),
        name="moe_router",
    )(x, g.reshape(1, d), router_p)


def _gather_norm_kernel(src_ref, used_ref, h_hbm, g_ref, xs_ref, buf, sem, *, sub):
    k = pl.program_id(0)

    def issue(kk, slot):
        base = kk * sub

        def body(r8, carry):
            for u in range(8):
                r = r8 * 8 + u
                tok = src_ref[base + r]
                pltpu.make_async_copy(h_hbm.at[pl.ds(tok, 1), :], buf.at[slot, pl.ds(r, 1), :],
                                      sem.at[slot]).start(priority=u % 2)
            return carry
        lax.fori_loop(0, sub // 8, body, 0)

    @pl.when(jnp.logical_and(k == 0, used_ref[0] == 1))
    def _():
        issue(0, 0)

    @pl.when(k + 1 < pl.num_programs(0))
    def _():
        @pl.when(used_ref[k + 1] == 1)
        def _():
            issue(k + 1, (k + 1) & 1)

    @pl.when(used_ref[k] == 1)
    def _():
        slot = k & 1
        pltpu.make_async_copy(h_hbm.at[pl.ds(0, sub), :], buf.at[slot], sem.at[slot]).wait()
        xs_ref[...] = _rms(buf[slot], g_ref[...]).astype(BF16)

    @pl.when(used_ref[k] == 0)
    def _():
        xs_ref[...] = jnp.zeros_like(xs_ref)


def _gather_norm(h, g, src_token, used, *, sub):
    _, d = h.shape
    grid_spec = pltpu.PrefetchScalarGridSpec(
        num_scalar_prefetch=2,
        grid=(used.shape[0],),
        in_specs=[pl.BlockSpec(memory_space=pl.ANY), pl.BlockSpec((1, d), lambda k, s, u: (0, 0))],
        out_specs=pl.BlockSpec((sub, d), lambda k, s, u: (k, 0)),
        scratch_shapes=[pltpu.VMEM((2, sub, d), F32), pltpu.SemaphoreType.DMA((2,))],
    )
    return pl.pallas_call(
        functools.partial(_gather_norm_kernel, sub=sub),
        grid_spec=grid_spec,
        out_shape=jax.ShapeDtypeStruct((src_token.shape[0], d), BF16),
        compiler_params=_cparams("arbitrary"),
        name="moe_gather",
    )(src_token, used, h, g.reshape(1, d))


def _combine_kernel(y_ref, h_ref, w_ref, g_ref, out_ref):
    d = h_ref.shape[1]
    w = w_ref[...]
    y = h_ref[...] + w[:, 0:1] * y_ref[:, 0:d] + w[:, 1:2] * y_ref[:, d:2 * d]
    out_ref[...] = _rms(y, g_ref[...])


def _combine(y_assign, h, wts, g, *, tm):
    m, d = h.shape
    y2 = y_assign.reshape(-1, TOP_K * d)
    return pl.pallas_call(
        _combine_kernel,
        grid=(m // tm,),
        in_specs=[
            pl.BlockSpec((tm, TOP_K * d), lambda i: (i, 0)),
            pl.BlockSpec((tm, d), lambda i: (i, 0)),
            pl.BlockSpec((tm, LANES), lambda i: (i, 0)),
            pl.BlockSpec((1, d), lambda i: (0, 0)),
        ],
        out_specs=pl.BlockSpec((tm, d), lambda i: (i, 0)),
        out_shape=jax.ShapeDtypeStruct((m, d), F32),
        compiler_params=_cparams("parallel"),
        name="moe_combine",
    )(y2, h, wts, g.reshape(1, d))


def _moe_ffn_kernel(te_ref, nsub_ref, nt_ref, dst_ref, x_ref, wg_ref, wu_ref, wd_ref, wgt_ref, wut_ref, wdt_ref,
                    y_hbm, acc, wgb, wub, wdb, wgtb, wutb, wdtb, sem, *, sub, n_assign):
    t = pl.program_id(0)
    j = pl.program_id(1)
    tile = acc.shape[0]
    ns = nsub_ref[t]

    def scatter_rows(start, size):
        def body(r8, carry):
            for u in range(8):
                r = start + r8 * 8 + u
                pltpu.make_async_copy(acc.at[pl.ds(r, 1), :], y_hbm.at[pl.ds(dst_ref[t * tile + r], 1), :],
                                      sem.at[0]).start(priority=u % 2)
            return carry
        lax.fori_loop(0, size // 8, body, 0)

    def block_copy(dst_row):
        return pltpu.make_async_copy(acc.at[pl.ds(0, sub), :], y_hbm.at[pl.ds(dst_row, sub), :], sem.at[0])

    def accumulate(wg, wu, wd, scatter):
        def rows_at(start, size):
            rows = pl.ds(pl.multiple_of(start, sub), size)
            x = x_ref[rows, :]
            h = _silu(_bdot(x, wg[...])) * _bdot(x, wu[...])
            acc[rows, :] += _bdot(h.astype(BF16), wd[...])
            if scatter:
                scatter_rows(start, size)

        def pair(p, carry):
            rows_at(p * (2 * sub), 2 * sub)
            return carry

        lax.fori_loop(0, lax.shift_right_logical(ns, 1), pair, 0)

        @pl.when((ns & 1) == 1)
        def _():
            rows_at((ns - 1) * sub, sub)

    @pl.when(t < nt_ref[0])
    def _():
        @pl.when(j == 0)
        def _():
            acc[...] = jnp.zeros_like(acc)

            @pl.when(t == 0)
            def _():
                spare = block_copy(n_assign)
                spare.start()
                spare.wait()

            wgtb[...] = wgt_ref[...].astype(BF16)
            wutb[...] = wut_ref[...].astype(BF16)
            wdtb[...] = wdt_ref[...].astype(BF16)
            accumulate(wgtb, wutb, wdtb, False)

        wgb[...] = wg_ref[...].astype(BF16)
        wub[...] = wu_ref[...].astype(BF16)
        wdb[...] = wd_ref[...].astype(BF16)
        last = j == pl.num_programs(1) - 1

        @pl.when(jnp.logical_not(last))
        def _():
            accumulate(wgb, wub, wdb, False)

        @pl.when(last)
        def _():
            accumulate(wgb, wub, wdb, True)

            def wait_block(s, carry):
                block_copy(0).wait()
                return carry
            lax.fori_loop(0, ns, wait_block, 0)


def _moe_ffn(xs, w_gate, w_up, w_down, plan, dst_row, *, tile, sub, tf, n_assign):
    npad, d = xs.shape
    nj = FF_MAIN // tf
    tail_blk = FF_MAIN // FF_TAIL

    def live(t, nt):
        return jnp.minimum(t, nt[0] - 1)

    def jj(t, j, nt):
        return jnp.where(t < nt[0], j, nj - 1)

    resident = dict(pipeline_mode=pl.Buffered(1))
    grid_spec = pltpu.PrefetchScalarGridSpec(
        num_scalar_prefetch=4,
        grid=(npad // tile, nj),
        in_specs=[
            pl.BlockSpec((tile, d), lambda t, j, te, ns, nt, ds: (live(t, nt), 0)),
            pl.BlockSpec((None, d, tf), lambda t, j, te, ns, nt, ds: (te[t], 0, jj(t, j, nt))),
            pl.BlockSpec((None, d, tf), lambda t, j, te, ns, nt, ds: (te[t], 0, jj(t, j, nt))),
            pl.BlockSpec((None, tf, d), lambda t, j, te, ns, nt, ds: (te[t], jj(t, j, nt), 0)),
            pl.BlockSpec((None, d, FF_TAIL), lambda t, j, te, ns, nt, ds: (te[t], 0, tail_blk), **resident),
            pl.BlockSpec((None, d, FF_TAIL), lambda t, j, te, ns, nt, ds: (te[t], 0, tail_blk), **resident),
            pl.BlockSpec((None, FF_TAIL, d), lambda t, j, te, ns, nt, ds: (te[t], tail_blk, 0), **resident),
        ],
        out_specs=pl.BlockSpec(memory_space=pl.ANY),
        scratch_shapes=[pltpu.VMEM((tile, d), F32),
                        pltpu.VMEM((d, tf), BF16), pltpu.VMEM((d, tf), BF16), pltpu.VMEM((tf, d), BF16),
                        pltpu.VMEM((d, FF_TAIL), BF16), pltpu.VMEM((d, FF_TAIL), BF16),
                        pltpu.VMEM((FF_TAIL, d), BF16), pltpu.SemaphoreType.DMA((1,))],
    )
    return pl.pallas_call(
        functools.partial(_moe_ffn_kernel, sub=sub, n_assign=n_assign),
        grid_spec=grid_spec,
        out_shape=jax.ShapeDtypeStruct((n_assign + sub, d), F32),
        compiler_params=_cparams("arbitrary", "arbitrary"),
        name="moe_ffn",
    )(*plan, dst_row, xs, w_gate, w_up, w_down, w_gate, w_up, w_down)


def _moe_plan(expert_idx, *, tile, sub, max_tiles):
    n = expert_idx.shape[0]
    flat = expert_idx.reshape(-1)
    onehot = (flat[:, None] == jnp.arange(N_EXPERTS, dtype=jnp.int32)[None, :]).astype(jnp.int32)
    incl = jnp.cumsum(onehot, axis=0)
    counts = incl[-1]
    rank = jnp.sum((incl - onehot) * onehot, axis=1)
    tiles_e = (counts + tile - 1) // tile
    tile_end = jnp.cumsum(tiles_e)
    tile_start = tile_end - tiles_e
    dest = jnp.sum(onehot * (tile_start * tile)[None, :], axis=1) + rank
    n_tiles = tile_end[-1:]
    tile_ids = jnp.arange(max_tiles, dtype=jnp.int32)
    te = jnp.sum((tile_ids[:, None] >= tile_end[None, :]).astype(jnp.int32), axis=1)
    last_e = jnp.sum((n_tiles - 1 >= tile_end).astype(jnp.int32))
    te = jnp.minimum(te, last_e)
    rows_used = jnp.clip(counts[te] - (tile_ids - tile_start[te]) * tile, 0, tile)
    nsub = jnp.where(tile_ids < n_tiles, (rows_used + sub - 1) // sub, 0).astype(jnp.int32)
    n_assign = n * TOP_K
    rows = jnp.arange(max_tiles * tile, dtype=jnp.int32)
    assign = jnp.full((max_tiles * tile,), -1, jnp.int32).at[dest].set(jnp.arange(n_assign, dtype=jnp.int32))
    src_token = jnp.maximum(assign, 0) // TOP_K
    dst_row = jnp.where(assign >= 0, assign, n_assign + rows % sub)
    used = (jnp.arange(tile // sub, dtype=jnp.int32)[None, :] < nsub[:, None]).reshape(-1).astype(jnp.int32)
    plan = (te.astype(jnp.int32), nsub, n_tiles.astype(jnp.int32))
    return src_token, dst_row, used, plan


def kernel(x, meta_tokens, a_norm, a_w_in, a_conv_w, a_conv_b, a_w_rgate, a_b_rgate, a_w_igate, a_b_igate,
           a_lambda, a_w_out, kv_norm, w_kv, b_norm, b_w_q, b_lambda_q1, b_lambda_k1, b_lambda_q2, b_lambda_k2,
           b_subln, b_w_out, ffn_norm, ffn_w_gate, ffn_w_up, ffn_w_down, moe_router, moe_w_gate, moe_w_up,
           moe_w_down, final_norm):
    bsz, seq, d = x.shape
    full = N_META + seq
    tm0 = 688
    tm0_big = 2 * tm0
    tm1_big = 1024
    h = jnp.concatenate([jnp.broadcast_to(meta_tokens.astype(x.dtype)[None], (bsz, N_META, d)), x], axis=1)
    h = h.reshape(bsz * full, d)

    xg = _norm_matmul(h, a_norm[0], a_w_in[0], tm=tm0_big, tn=512, out_dtype=F32)
    y = _rglru(xg.reshape(bsz, full, 2 * d), a_conv_w[0], a_conv_b[0], a_w_rgate[0], a_b_rgate[0],
               a_w_igate[0], a_b_igate[0], a_lambda[0], c=256)
    h = _matmul_res(y.reshape(bsz * full, d), a_w_out[0], h, tm=tm0_big, tn=512)
    h = _ffn(h, ffn_norm[0], ffn_w_gate[0], ffn_w_up[0], ffn_w_down[0], tm=tm0_big, tf=256)

    kv = _norm_matmul(h, kv_norm, w_kv, tm=tm0_big, tn=512, out_dtype=BF16).reshape(bsz, full, -1)
    h = h.reshape(bsz, full, d)[:, N_META:].reshape(bsz * seq, d)

    lambda_init = 0.8 - 0.6 * math.exp(-0.3 * 1)
    q = _norm_matmul(h, b_norm[0], b_w_q[0], tm=tm1_big, tn=512, out_dtype=BF16, scale=HEAD_DIM ** -0.5)
    o = _diff_attention(q.reshape(bsz, seq, -1), kv, b_lambda_q1[0], b_lambda_k1[0], b_lambda_q2[0],
                        b_lambda_k2[0], b_subln[0], tq=256, lambda_init=lambda_init)
    h = _matmul_res(o.reshape(bsz * seq, -1), b_w_out[0], h, tm=tm1_big, tn=512)

    n_tok = bsz * seq
    max_tiles = n_tok * TOP_K // MOE_TILE + N_EXPERTS
    idx, wts = _router(h, ffn_norm[1], moe_router[0], tm=512)
    src_token, dst_row, used, plan = _moe_plan(idx[:, :TOP_K], tile=MOE_TILE, sub=MOE_SUB, max_tiles=max_tiles)
    xs = _gather_norm(h, ffn_norm[1], src_token, used, sub=MOE_SUB)
    y_assign = _moe_ffn(xs, moe_w_gate[0], moe_w_up[0], moe_w_down[0], plan, dst_row, tile=MOE_TILE, sub=MOE_SUB,
                        tf=256, n_assign=n_tok * TOP_K)
    out = _combine(y_assign, h, wts, final_norm, tm=256)
    return out.reshape(bsz, seq, d)
```

```python
import functools
import math

import jax
import jax.numpy as jnp
from jax import lax
from jax.experimental import pallas as pl
from jax.experimental.pallas import tpu as pltpu

F32 = jnp.float32
BF16 = jnp.bfloat16

D_MODEL = 2048
N_META = 16
CONV_WIDTH = 4
RGLRU_C = 8.0
RNN_BLOCK = 128
HEAD_DIM = 128
N_HEADS = 8
V_HEAD_DIM = 2 * HEAD_DIM
D_FF = 5504
N_EXPERTS = 8
TOP_K = 2
EPS = 1e-6

LANES = 128
FF_TAIL = LANES
FF_MAIN = D_FF - FF_TAIL
VMEM_LIMIT = 58 * 1024 * 1024
MOE_SUB = 256
MOE_TILE = 9 * MOE_SUB


def _cparams(*sem):
    return pltpu.CompilerParams(dimension_semantics=sem, vmem_limit_bytes=VMEM_LIMIT)


def _rms(x, g):
    return x * lax.rsqrt(jnp.mean(x * x, axis=-1, keepdims=True) + EPS) * g


def _sigmoid(x):
    return 0.5 * jnp.tanh(0.5 * x) + 0.5


def _silu(x):
    return x * _sigmoid(x)


def _gelu_tanh(x):
    c = math.sqrt(2.0 / math.pi)
    return x * (0.5 * jnp.tanh(x * (c + (c * 0.044715) * (x * x))) + 0.5)


def _bdot(a, b):
    return jnp.dot(a, b, preferred_element_type=F32)


def _norm_matmul_kernel(x_ref, g_ref, w_ref, o_ref, xn_ref, *, scale):
    @pl.when(pl.program_id(1) == 0)
    def _():
        xn_ref[...] = _rms(x_ref[...], g_ref[...]).astype(BF16)

    acc = _bdot(xn_ref[...], w_ref[...].astype(BF16))
    if scale != 1.0:
        acc = acc * scale
    o_ref[...] = acc.astype(o_ref.dtype)


def _norm_matmul(x, g, w, *, tm, tn, out_dtype, scale=1.0):
    m, k = x.shape
    n = w.shape[1]
    return pl.pallas_call(
        functools.partial(_norm_matmul_kernel, scale=scale),
        grid=(m // tm, n // tn),
        in_specs=[
            pl.BlockSpec((tm, k), lambda i, j: (i, 0)),
            pl.BlockSpec((1, k), lambda i, j: (0, 0)),
            pl.BlockSpec((k, tn), lambda i, j: (0, j)),
        ],
        out_specs=pl.BlockSpec((tm, tn), lambda i, j: (i, j)),
        out_shape=jax.ShapeDtypeStruct((m, n), out_dtype),
        scratch_shapes=[pltpu.VMEM((tm, k), BF16)],
        compiler_params=_cparams("parallel", "arbitrary"),
        name="norm_matmul",
    )(x, g.reshape(1, k), w)


def _matmul_res_kernel(y_ref, w_ref, r_ref, o_ref):
    o_ref[...] = r_ref[...] + _bdot(y_ref[...], w_ref[...].astype(BF16))


def _matmul_res(y, w, res, *, tm, tn):
    m, k = y.shape
    n = w.shape[1]
    return pl.pallas_call(
        _matmul_res_kernel,
        grid=(m // tm, n // tn),
        in_specs=[
            pl.BlockSpec((tm, k), lambda i, j: (i, 0)),
            pl.BlockSpec((k, tn), lambda i, j: (0, j)),
            pl.BlockSpec((tm, tn), lambda i, j: (i, j)),
        ],
        out_specs=pl.BlockSpec((tm, tn), lambda i, j: (i, j)),
        out_shape=jax.ShapeDtypeStruct((m, n), F32),
        compiler_params=_cparams("parallel", "arbitrary"),
        name="matmul_res",
    )(y, w, res)


def _rglru_kernel(x_ref, g_ref, cw_ref, cb_ref, wr_ref, br_ref, wi_ref, bi_ref, lam_ref, y_ref, a_s, b_s):
    seq, c = a_s.shape
    x = x_ref[0]
    row = lax.broadcasted_iota(jnp.int32, (seq, c), 0)
    xc = jnp.broadcast_to(cb_ref[...], (seq, c))
    for k in range(CONV_WIDTH):
        shift = CONV_WIDTH - 1 - k
        xs = x if shift == 0 else jnp.where(row >= shift, pltpu.roll(x, shift, 0), 0.0)
        xc = xc + xs * cw_ref[k:k + 1, :]

    lam = lam_ref[...]
    softplus_neg_lam = jnp.maximum(-lam, 0.0) + jnp.log1p(jnp.exp(-jnp.abs(lam)))
    for gi in range(c // RNN_BLOCK):
        sl = slice(gi * RNN_BLOCK, (gi + 1) * RNN_BLOCK)
        xcg = xc[:, sl]
        xb = xcg.astype(BF16)
        r = _sigmoid(_bdot(xb, wr_ref[gi].astype(BF16)) + br_ref[:, sl])
        ig = _sigmoid(_bdot(xb, wi_ref[gi].astype(BF16)) + bi_ref[:, sl])
        log_a = (-RGLRU_C) * r * softplus_neg_lam[:, sl]
        a = jnp.exp(log_a)
        a_s[:, sl] = a
        b_s[:, sl] = jnp.sqrt(jnp.tanh(-log_a) * (1.0 + a * a)) * (ig * xcg)

    sub = lax.broadcasted_iota(jnp.int32, (8, c), 0)
    tiles_per_trip = 6
    rows_per_trip = 8 * tiles_per_trip
    assert seq % rows_per_trip == 0

    def trip(i, h_prev):
        r0 = pl.multiple_of(i * rows_per_trip, 8)
        scanned = []
        for u in range(tiles_per_trip):
            a = a_s[pl.ds(r0 + 8 * u, 8), :]
            b = b_s[pl.ds(r0 + 8 * u, 8), :]
            for s in (1, 2, 4):
                keep = sub >= s
                a_sh = jnp.where(keep, pltpu.roll(a, s, 0), 1.0)
                b_sh = jnp.where(keep, pltpu.roll(b, s, 0), 0.0)
                b = a * b_sh + b
                a = a * a_sh
            scanned.append((a, b))
        for u, (a, b) in enumerate(scanned):
            h = a * h_prev + b
            b_s[pl.ds(r0 + 8 * u, 8), :] = h
            h_prev = jnp.broadcast_to(h[7:8, :], (8, c))
        return h_prev

    lax.fori_loop(0, seq // rows_per_trip, trip, jnp.zeros((8, c), F32))
    y_ref[0] = (b_s[...] * _gelu_tanh(g_ref[0])).astype(BF16)


def _rglru(xg, conv_w, conv_b, w_rg, b_rg, w_ig, b_ig, lam, *, c):
    bsz, seq, two_d = xg.shape
    d = two_d // 2
    nc = d // c
    gpb = c // RNN_BLOCK
    vec = lambda v: v.reshape(1, d)
    vspec = pl.BlockSpec((1, c), lambda b, j: (0, j))
    return pl.pallas_call(
        _rglru_kernel,
        grid=(bsz, nc),
        in_specs=[
            pl.BlockSpec((1, seq, c), lambda b, j: (b, 0, j)),
            pl.BlockSpec((1, seq, c), lambda b, j: (b, 0, nc + j)),
            pl.BlockSpec((CONV_WIDTH, c), lambda b, j: (0, j)),
            vspec,
            pl.BlockSpec((gpb, RNN_BLOCK, RNN_BLOCK), lambda b, j: (j, 0, 0)),
            vspec,
            pl.BlockSpec((gpb, RNN_BLOCK, RNN_BLOCK), lambda b, j: (j, 0, 0)),
            vspec,
            vspec,
        ],
        out_specs=pl.BlockSpec((1, seq, c), lambda b, j: (b, 0, j)),
        out_shape=jax.ShapeDtypeStruct((bsz, seq, d), BF16),
        scratch_shapes=[pltpu.VMEM((seq, c), F32), pltpu.VMEM((seq, c), F32)],
        compiler_params=_cparams("parallel", "parallel"),
        name="rglru",
    )(xg, xg, conv_w, vec(conv_b), w_rg, vec(b_rg), w_ig, vec(b_ig), vec(lam))


def _ffn_kernel(x_ref, g_ref, wg_ref, wu_ref, wd_ref, wgt_ref, wut_ref, wdt_ref, o_ref, xn_ref):
    j = pl.program_id(1)

    def contrib(wg, wu, wd):
        xn = xn_ref[...]
        h = _silu(_bdot(xn, wg[...].astype(BF16))) * _bdot(xn, wu[...].astype(BF16))
        return _bdot(h.astype(BF16), wd[...].astype(BF16))

    @pl.when(j == 0)
    def _():
        x = x_ref[...]
        xn_ref[...] = _rms(x, g_ref[...]).astype(BF16)
        o_ref[...] = x + contrib(wgt_ref, wut_ref, wdt_ref)

    o_ref[...] += contrib(wg_ref, wu_ref, wd_ref)


def _ffn(x, g, w_gate, w_up, w_down, *, tm, tf):
    m, d = x.shape
    tail_blk = FF_MAIN // FF_TAIL
    return pl.pallas_call(
        _ffn_kernel,
        grid=(m // tm, FF_MAIN // tf),
        in_specs=[
            pl.BlockSpec((tm, d), lambda i, j: (i, 0), pipeline_mode=pl.Buffered(1)),
            pl.BlockSpec((1, d), lambda i, j: (0, 0)),
            pl.BlockSpec((d, tf), lambda i, j: (0, j)),
            pl.BlockSpec((d, tf), lambda i, j: (0, j)),
            pl.BlockSpec((tf, d), lambda i, j: (j, 0)),
            pl.BlockSpec((d, FF_TAIL), lambda i, j: (0, tail_blk)),
            pl.BlockSpec((d, FF_TAIL), lambda i, j: (0, tail_blk)),
            pl.BlockSpec((FF_TAIL, d), lambda i, j: (tail_blk, 0)),
        ],
        out_specs=pl.BlockSpec((tm, d), lambda i, j: (i, 0), pipeline_mode=pl.Buffered(1)),
        out_shape=jax.ShapeDtypeStruct((m, d), F32),
        scratch_shapes=[pltpu.VMEM((tm, d), BF16)],
        compiler_params=_cparams("parallel", "arbitrary"),
        name="dense_ffn",
    )(x, g.reshape(1, d), w_gate, w_up, w_down, w_gate, w_up, w_down)


def _attn_kernel(slopes_ref, q1_ref, q2_ref, k1_ref, k2_ref, v_ref, lq1_ref, lk1_ref, lq2_ref, lk2_ref,
                 sg_ref, o_ref, *, tq, lambda_init):
    slope = slopes_ref[pl.program_id(1)]
    seq = q1_ref.shape[1]
    q_refs = (q1_ref, q2_ref)
    k_refs = (k1_ref, k2_ref)
    nt = (((1,), (1,)), ((), ()))
    lam = (jnp.exp(jnp.sum(lq1_ref[...] * lk1_ref[...], axis=-1, keepdims=True))
           - jnp.exp(jnp.sum(lq2_ref[...] * lk2_ref[...], axis=-1, keepdims=True)) + lambda_init)
    lane = lax.broadcasted_iota(jnp.int32, (1, LANES), 1)
    meta_mask = lane < N_META
    col = lax.broadcasted_iota(jnp.int32, (1, tq), 1)
    diag_mask = col <= lax.broadcasted_iota(jnp.int32, (tq, tq), 0)
    diag_bias = slope * col.astype(F32)

    for qi in range(seq // tq):
        r0 = qi * tq
        pieces = [(0, LANES, slope * (lane - (N_META + r0)).astype(F32), meta_mask),
                  (N_META + r0, tq, diag_bias, diag_mask)]
        if qi > 0:
            off_col = lax.broadcasted_iota(jnp.int32, (1, r0), 1)
            pieces.append((N_META, r0, slope * (off_col - r0).astype(F32), None))

        exps, scales = [], []
        for s in range(2):
            q = q_refs[s][0, r0:r0 + tq, :]
            scores = []
            for k0, rows, bias, mask in pieces:
                sc = lax.dot_general(q, k_refs[s][0, k0:k0 + rows, :], nt, preferred_element_type=F32) + bias
                scores.append(sc if mask is None else jnp.where(mask, sc, -jnp.inf))
            m = functools.reduce(jnp.maximum, [jnp.max(sc, axis=-1, keepdims=True) for sc in scores])
            e = [jnp.exp(sc - m) for sc in scores]
            l = functools.reduce(jnp.add, [jnp.sum(x, axis=-1, keepdims=True) for x in e])
            exps.append(e)
            scales.append(1.0 / l if s == 0 else lam / l)

        o = None
        for (k0, rows, _, _), e1, e2 in zip(pieces, exps[0], exps[1]):
            p = (e1 * scales[0] - e2 * scales[1]).astype(BF16)
            pv = _bdot(p, v_ref[0, k0:k0 + rows, :])
            o = pv if o is None else o + pv
        o_ref[0, r0:r0 + tq, :] = (_rms(o, sg_ref[...]) * (1.0 - lambda_init)).astype(o_ref.dtype)


def _diff_attention(q, kv, lq1, lk1, lq2, lk2, subln_g, *, tq, lambda_init):
    bsz, seq, _ = q.shape
    full = kv.shape[1]
    slopes = jnp.asarray([2.0 ** (-8.0 * (i + 1) / N_HEADS) for i in range(N_HEADS)], F32)
    vec = lambda v: v.reshape(1, -1)
    lspec = pl.BlockSpec((1, HEAD_DIM), lambda b, h, s: (0, 0))
    grid_spec = pltpu.PrefetchScalarGridSpec(
        num_scalar_prefetch=1,
        grid=(bsz, N_HEADS),
        in_specs=[
            pl.BlockSpec((1, seq, HEAD_DIM), lambda b, h, s: (b, 0, h)),
            pl.BlockSpec((1, seq, HEAD_DIM), lambda b, h, s: (b, 0, N_HEADS + h)),
            pl.BlockSpec((1, full, HEAD_DIM), lambda b, h, s: (b, 0, h)),
            pl.BlockSpec((1, full, HEAD_DIM), lambda b, h, s: (b, 0, N_HEADS + h)),
            pl.BlockSpec((1, full, V_HEAD_DIM), lambda b, h, s: (b, 0, N_HEADS + h)),
            lspec, lspec, lspec, lspec,
            pl.BlockSpec((1, V_HEAD_DIM), lambda b, h, s: (0, 0)),
        ],
        out_specs=pl.BlockSpec((1, seq, V_HEAD_DIM), lambda b, h, s: (b, 0, h)),
    )
    return pl.pallas_call(
        functools.partial(_attn_kernel, tq=tq, lambda_init=lambda_init),
        grid_spec=grid_spec,
        out_shape=jax.ShapeDtypeStruct((bsz, seq, N_HEADS * V_HEAD_DIM), BF16),
        compiler_params=_cparams("parallel", "parallel"),
        name="diff_attention",
    )(slopes, q, q, kv, kv, kv, vec(lq1), vec(lk1), vec(lq2), vec(lk2), vec(subln_g))


def _router_kernel(x_ref, g_ref, r_ref, idx_ref, w_ref):
    u = _rms(x_ref[...], g_ref[...])
    logits = jnp.dot(u, r_ref[...], precision=lax.Precision.HIGHEST, preferred_element_type=F32)
    lane = lax.broadcasted_iota(jnp.int32, logits.shape, 1).astype(F32)
    lg = jnp.where(lane < N_EXPERTS, logits, -jnp.inf)
    m1 = jnp.max(lg, axis=-1, keepdims=True)
    i1 = jnp.min(jnp.where(lg == m1, lane, float(LANES)), axis=-1, keepdims=True)
    lg2 = jnp.where(lane == i1, -jnp.inf, lg)
    m2 = jnp.max(lg2, axis=-1, keepdims=True)
    i2 = jnp.min(jnp.where(lg2 == m2, lane, float(LANES)), axis=-1, keepdims=True)
    e2 = jnp.exp(m2 - m1)
    w1 = 1.0 / (1.0 + e2)
    w2 = e2 / (1.0 + e2)
    idx_ref[...] = jnp.where(lane == 0.0, i1, jnp.where(lane == 1.0, i2, 0.0)).astype(jnp.int32)
    w_ref[...] = jnp.where(lane == 0.0, w1, jnp.where(lane == 1.0, w2, 0.0))


def _router(x, g, router, *, tm):
    m, d = x.shape
    router_p = jnp.pad(router, ((0, 0), (0, LANES - N_EXPERTS)))
    out = pl.BlockSpec((tm, LANES), lambda i: (i, 0))
    return pl.pallas_call(
        _router_kernel,
        grid=(m // tm,),
        in_specs=[
            pl.BlockSpec((tm, d), lambda i: (i, 0)),
            pl.BlockSpec((1, d), lambda i: (0, 0)),
            pl.BlockSpec((d, LANES), lambda i: (0, 0)),
        ],
        out_specs=[out, out],
        out_shape=[jax.ShapeDtypeStruct((m, LANES), jnp.int32), jax.ShapeDtypeStruct((m, LANES), F32)],
        compiler_params=_cparams("parallel"),
        name="moe_router",
    )(x, g.reshape(1, d), router_p)


def _gather_norm_kernel(src_ref, used_ref, h_hbm, g_ref, xs_ref, buf, sem, *, sub):
    k = pl.program_id(0)

    def issue(kk, slot):
        base = kk * sub

        def body(r8, carry):
            for u in range(8):
                r = r8 * 8 + u
                tok = src_ref[base + r]
                pltpu.make_async_copy(h_hbm.at[pl.ds(tok, 1), :], buf.at[slot, pl.ds(r, 1), :],
                                      sem.at[slot]).start(priority=u % 2)
            return carry
        lax.fori_loop(0, sub // 8, body, 0)

    @pl.when(jnp.logical_and(k == 0, used_ref[0] == 1))
    def _():
        issue(0, 0)

    @pl.when(k + 1 < pl.num_programs(0))
    def _():
        @pl.when(used_ref[k + 1] == 1)
        def _():
            issue(k + 1, (k + 1) & 1)

    @pl.when(used_ref[k] == 1)
    def _():
        slot = k & 1
        pltpu.make_async_copy(h_hbm.at[pl.ds(0, sub), :], buf.at[slot], sem.at[slot]).wait()
        xs_ref[...] = _rms(buf[slot], g_ref[...]).astype(BF16)

    @pl.when(used_ref[k] == 0)
    def _():
        xs_ref[...] = jnp.zeros_like(xs_ref)


def _gather_norm(h, g, src_token, used, *, sub):
    _, d = h.shape
    grid_spec = pltpu.PrefetchScalarGridSpec(
        num_scalar_prefetch=2,
        grid=(used.shape[0],),
        in_specs=[pl.BlockSpec(memory_space=pl.ANY), pl.BlockSpec((1, d), lambda k, s, u: (0, 0))],
        out_specs=pl.BlockSpec((sub, d), lambda k, s, u: (k, 0)),
        scratch_shapes=[pltpu.VMEM((2, sub, d), F32), pltpu.SemaphoreType.DMA((2,))],
    )
    return pl.pallas_call(
        functools.partial(_gather_norm_kernel, sub=sub),
        grid_spec=grid_spec,
        out_shape=jax.ShapeDtypeStruct((src_token.shape[0], d), BF16),
        compiler_params=_cparams("arbitrary"),
        name="moe_gather",
    )(src_token, used, h, g.reshape(1, d))


def _combine_kernel(y0_ref, y1_ref, h_ref, w_ref, g_ref, out_ref):
    w = w_ref[...]
    y = h_ref[...] + w[:, 0:1] * y0_ref[...] + w[:, 1:2] * y1_ref[...]
    out_ref[...] = _rms(y, g_ref[...])


def _combine(y_assign, h, wts, g, *, tm):
    m, d = h.shape
    nb = m // tm
    return pl.pallas_call(
        _combine_kernel,
        grid=(nb,),
        in_specs=[
            pl.BlockSpec((tm, d), lambda i: (i, 0)),
            pl.BlockSpec((tm, d), lambda i: (nb + i, 0)),
            pl.BlockSpec((tm, d), lambda i: (i, 0)),
            pl.BlockSpec((tm, LANES), lambda i: (i, 0)),
            pl.BlockSpec((1, d), lambda i: (0, 0)),
        ],
        out_specs=pl.BlockSpec((tm, d), lambda i: (i, 0)),
        out_shape=jax.ShapeDtypeStruct((m, d), F32),
        compiler_params=_cparams("parallel"),
        name="moe_combine",
    )(y_assign, y_assign, h, wts, g.reshape(1, d))


def _moe_ffn_kernel(te_ref, nsub_ref, nt_ref, dst_ref, x_ref, wg_ref, wu_ref, wd_ref, wgt_ref, wut_ref, wdt_ref,
                    y_hbm, acc, wgb, wub, wdb, wgtb, wutb, wdtb, sem, *, sub, n_assign):
    t = pl.program_id(0)
    j = pl.program_id(1)
    tile = acc.shape[0]
    ns = nsub_ref[t]

    def scatter_rows(start, size):
        def body(r8, carry):
            for u in range(8):
                r = start + r8 * 8 + u
                pltpu.make_async_copy(acc.at[pl.ds(r, 1), :], y_hbm.at[pl.ds(dst_ref[t * tile + r], 1), :],
                                      sem.at[0]).start(priority=u % 2)
            return carry
        lax.fori_loop(0, size // 8, body, 0)

    def block_copy(dst_row):
        return pltpu.make_async_copy(acc.at[pl.ds(0, sub), :], y_hbm.at[pl.ds(dst_row, sub), :], sem.at[0])

    def accumulate(wg, wu, wd, scatter):
        def rows_at(start, size):
            rows = pl.ds(pl.multiple_of(start, sub), size)
            x = x_ref[rows, :]
            h = _silu(_bdot(x, wg[...])) * _bdot(x, wu[...])
            acc[rows, :] += _bdot(h.astype(BF16), wd[...])
            if scatter:
                scatter_rows(start, size)

        def pair(p, carry):
            rows_at(p * (2 * sub), 2 * sub)
            return carry

        lax.fori_loop(0, lax.shift_right_logical(ns, 1), pair, 0)

        @pl.when((ns & 1) == 1)
        def _():
            rows_at((ns - 1) * sub, sub)

    @pl.when(t < nt_ref[0])
    def _():
        @pl.when(j == 0)
        def _():
            acc[...] = jnp.zeros_like(acc)

            @pl.when(t == 0)
            def _():
                spare = block_copy(n_assign)
                spare.start()
                spare.wait()

            wgtb[...] = wgt_ref[...].astype(BF16)
            wutb[...] = wut_ref[...].astype(BF16)
            wdtb[...] = wdt_ref[...].astype(BF16)
            accumulate(wgtb, wutb, wdtb, False)

        wgb[...] = wg_ref[...].astype(BF16)
        wub[...] = wu_ref[...].astype(BF16)
        wdb[...] = wd_ref[...].astype(BF16)
        last = j == pl.num_programs(1) - 1

        @pl.when(jnp.logical_not(last))
        def _():
            accumulate(wgb, wub, wdb, False)

        @pl.when(last)
        def _():
            accumulate(wgb, wub, wdb, True)

            def wait_block(s, carry):
                block_copy(0).wait()
                return carry
            lax.fori_loop(0, ns, wait_block, 0)


def _moe_ffn(xs, w_gate, w_up, w_down, plan, dst_row, *, tile, sub, tf, n_assign):
    npad, d = xs.shape
    nj = FF_MAIN // tf
    tail_blk = FF_MAIN // FF_TAIL

    def live(t, nt):
        return jnp.minimum(t, nt[0] - 1)

    def jj(t, j, nt):
        return jnp.where(t < nt[0], j, nj - 1)

    resident = dict(pipeline_mode=pl.Buffered(1))
    grid_spec = pltpu.PrefetchScalarGridSpec(
        num_scalar_prefetch=4,
        grid=(npad // tile, nj),
        in_specs=[
            pl.BlockSpec((tile, d), lambda t, j, te, ns, nt, ds: (live(t, nt), 0)),
            pl.BlockSpec((None, d, tf), lambda t, j, te, ns, nt, ds: (te[t], 0, jj(t, j, nt))),
            pl.BlockSpec((None, d, tf), lambda t, j, te, ns, nt, ds: (te[t], 0, jj(t, j, nt))),
            pl.BlockSpec((None, tf, d), lambda t, j, te, ns, nt, ds: (te[t], jj(t, j, nt), 0)),
            pl.BlockSpec((None, d, FF_TAIL), lambda t, j, te, ns, nt, ds: (te[t], 0, tail_blk), **resident),
            pl.BlockSpec((None, d, FF_TAIL), lambda t, j, te, ns, nt, ds: (te[t], 0, tail_blk), **resident),
            pl.BlockSpec((None, FF_TAIL, d), lambda t, j, te, ns, nt, ds: (te[t], tail_blk, 0), **resident),
        ],
        out_specs=pl.BlockSpec(memory_space=pl.ANY),
        scratch_shapes=[pltpu.VMEM((tile, d), F32),
                        pltpu.VMEM((d, tf), BF16), pltpu.VMEM((d, tf), BF16), pltpu.VMEM((tf, d), BF16),
                        pltpu.VMEM((d, FF_TAIL), BF16), pltpu.VMEM((d, FF_TAIL), BF16),
                        pltpu.VMEM((FF_TAIL, d), BF16), pltpu.SemaphoreType.DMA((1,))],
    )
    return pl.pallas_call(
        functools.partial(_moe_ffn_kernel, sub=sub, n_assign=n_assign),
        grid_spec=grid_spec,
        out_shape=jax.ShapeDtypeStruct((n_assign + sub, d), F32),
        compiler_params=_cparams("arbitrary", "arbitrary"),
        name="moe_ffn",
    )(*plan, dst_row, xs, w_gate, w_up, w_down, w_gate, w_up, w_down)


def _moe_plan(expert_idx, *, tile, sub, max_tiles):
    n = expert_idx.shape[0]
    flat = expert_idx.reshape(-1)
    onehot = (flat[:, None] == jnp.arange(N_EXPERTS, dtype=jnp.int32)[None, :]).astype(jnp.int32)
    incl = jnp.cumsum(onehot, axis=0)
    counts = incl[-1]
    rank = jnp.sum((incl - onehot) * onehot, axis=1)
    tiles_e = (counts + tile - 1) // tile
    tile_end = jnp.cumsum(tiles_e)
    tile_start = tile_end - tiles_e
    dest = jnp.sum(onehot * (tile_start * tile)[None, :], axis=1) + rank
    n_tiles = tile_end[-1:]
    tile_ids = jnp.arange(max_tiles, dtype=jnp.int32)
    te = jnp.sum((tile_ids[:, None] >= tile_end[None, :]).astype(jnp.int32), axis=1)
    last_e = jnp.sum((n_tiles - 1 >= tile_end).astype(jnp.int32))
    te = jnp.minimum(te, last_e)
    rows_used = jnp.clip(counts[te] - (tile_ids - tile_start[te]) * tile, 0, tile)
    nsub = jnp.where(tile_ids < n_tiles, (rows_used + sub - 1) // sub, 0).astype(jnp.int32)
    n_assign = n * TOP_K
    rows = jnp.arange(max_tiles * tile, dtype=jnp.int32)
    assign = jnp.full((max_tiles * tile,), -1, jnp.int32).at[dest].set(jnp.arange(n_assign, dtype=jnp.int32))
    src_token = jnp.maximum(assign, 0) // TOP_K
    dst_row = jnp.where(assign >= 0, (assign % TOP_K) * n + assign // TOP_K, n_assign + rows % sub)
    used = (jnp.arange(tile // sub, dtype=jnp.int32)[None, :] < nsub[:, None]).reshape(-1).astype(jnp.int32)
    plan = (te.astype(jnp.int32), nsub, n_tiles.astype(jnp.int32))
    return src_token, dst_row, used, plan


def kernel(x, meta_tokens, a_norm, a_w_in, a_conv_w, a_conv_b, a_w_rgate, a_b_rgate, a_w_igate, a_b_igate,
           a_lambda, a_w_out, kv_norm, w_kv, b_norm, b_w_q, b_lambda_q1, b_lambda_k1, b_lambda_q2, b_lambda_k2,
           b_subln, b_w_out, ffn_norm, ffn_w_gate, ffn_w_up, ffn_w_down, moe_router, moe_w_gate, moe_w_up,
           moe_w_down, final_norm):
    bsz, seq, d = x.shape
    full = N_META + seq
    tm0 = 688
    tm0_big = 2 * tm0
    tm1_big = 1024
    h = jnp.concatenate([jnp.broadcast_to(meta_tokens.astype(x.dtype)[None], (bsz, N_META, d)), x], axis=1)
    h = h.reshape(bsz * full, d)

    xg = _norm_matmul(h, a_norm[0], a_w_in[0], tm=tm0_big, tn=512, out_dtype=F32)
    y = _rglru(xg.reshape(bsz, full, 2 * d), a_conv_w[0], a_conv_b[0], a_w_rgate[0], a_b_rgate[0],
               a_w_igate[0], a_b_igate[0], a_lambda[0], c=256)
    h = _matmul_res(y.reshape(bsz * full, d), a_w_out[0], h, tm=tm0_big, tn=512)
    h = _ffn(h, ffn_norm[0], ffn_w_gate[0], ffn_w_up[0], ffn_w_down[0], tm=tm0_big, tf=256)

    kv = _norm_matmul(h, kv_norm, w_kv, tm=tm0_big, tn=512, out_dtype=BF16).reshape(bsz, full, -1)
    h = h.reshape(bsz, full, d)[:, N_META:].reshape(bsz * seq, d)

    lambda_init = 0.8 - 0.6 * math.exp(-0.3 * 1)
    q = _norm_matmul(h, b_norm[0], b_w_q[0], tm=tm1_big, tn=512, out_dtype=BF16, scale=HEAD_DIM ** -0.5)
    o = _diff_attention(q.reshape(bsz, seq, -1), kv, b_lambda_q1[0], b_lambda_k1[0], b_lambda_q2[0],
                        b_lambda_k2[0], b_subln[0], tq=256, lambda_init=lambda_init)
    h = _matmul_res(o.reshape(bsz * seq, -1), b_w_out[0], h, tm=tm1_big, tn=512)

    n_tok = bsz * seq
    max_tiles = n_tok * TOP_K // MOE_TILE + N_EXPERTS
    idx, wts = _router(h, ffn_norm[1], moe_router[0], tm=512)
    src_token, dst_row, used, plan = _moe_plan(idx[:, :TOP_K], tile=MOE_TILE, sub=MOE_SUB, max_tiles=max_tiles)
    xs = _gather_norm(h, ffn_norm[1], src_token, used, sub=MOE_SUB)
    y_assign = _moe_ffn(xs, moe_w_gate[0], moe_w_up[0], moe_w_down[0], plan, dst_row, tile=MOE_TILE, sub=MOE_SUB,
                        tf=256, n_assign=n_tok * TOP_K)
    out = _combine(y_assign, h, wts, final_norm, tm=256)
    return out.reshape(bsz, seq, d)
```

```python
import functools
import math

import jax
import jax.numpy as jnp
from jax import lax
from jax.experimental import pallas as pl
from jax.experimental.pallas import tpu as pltpu

F32 = jnp.float32
BF16 = jnp.bfloat16

D_MODEL = 2048
N_META = 16
CONV_WIDTH = 4
RGLRU_C = 8.0
RNN_BLOCK = 128
HEAD_DIM = 128
N_HEADS = 8
V_HEAD_DIM = 2 * HEAD_DIM
D_FF = 5504
N_EXPERTS = 8
TOP_K = 2
EPS = 1e-6

LOG2E = math.log2(math.e)
LANES = 128
FF_TAIL = LANES
FF_MAIN = D_FF - FF_TAIL
VMEM_LIMIT = 58 * 1024 * 1024
MOE_SUB = 256
MOE_TILE = 9 * MOE_SUB


def _cparams(*sem):
    return pltpu.CompilerParams(dimension_semantics=sem, vmem_limit_bytes=VMEM_LIMIT)


def _rms(x, g):
    return x * lax.rsqrt(jnp.mean(x * x, axis=-1, keepdims=True) + EPS) * g


def _sigmoid(x):
    return 0.5 * jnp.tanh(0.5 * x) + 0.5


def _silu(x):
    return x * _sigmoid(x)


def _gelu_tanh(x):
    c = math.sqrt(2.0 / math.pi)
    return x * (0.5 * jnp.tanh(x * (c + (c * 0.044715) * (x * x))) + 0.5)


def _bdot(a, b):
    return jnp.dot(a, b, preferred_element_type=F32)


def _norm_matmul_kernel(x_ref, g_ref, w_ref, o_ref, xn_ref, *, scale):
    @pl.when(pl.program_id(1) == 0)
    def _():
        xn_ref[...] = _rms(x_ref[...], g_ref[...]).astype(BF16)

    acc = _bdot(xn_ref[...], w_ref[...].astype(BF16))
    if scale != 1.0:
        acc = acc * scale
    o_ref[...] = acc.astype(o_ref.dtype)


def _norm_matmul(x, g, w, *, tm, tn, out_dtype, scale=1.0):
    m, k = x.shape
    n = w.shape[1]
    return pl.pallas_call(
        functools.partial(_norm_matmul_kernel, scale=scale),
        grid=(m // tm, n // tn),
        in_specs=[
            pl.BlockSpec((tm, k), lambda i, j: (i, 0)),
            pl.BlockSpec((1, k), lambda i, j: (0, 0)),
            pl.BlockSpec((k, tn), lambda i, j: (0, j)),
        ],
        out_specs=pl.BlockSpec((tm, tn), lambda i, j: (i, j)),
        out_shape=jax.ShapeDtypeStruct((m, n), out_dtype),
        scratch_shapes=[pltpu.VMEM((tm, k), BF16)],
        compiler_params=_cparams("parallel", "arbitrary"),
        name="norm_matmul",
    )(x, g.reshape(1, k), w)


def _matmul_res_kernel(y_ref, w_ref, r_ref, o_ref):
    o_ref[...] = r_ref[...] + _bdot(y_ref[...], w_ref[...].astype(BF16))


def _matmul_res(y, w, res, *, tm, tn):
    m, k = y.shape
    n = w.shape[1]
    return pl.pallas_call(
        _matmul_res_kernel,
        grid=(m // tm, n // tn),
        in_specs=[
            pl.BlockSpec((tm, k), lambda i, j: (i, 0)),
            pl.BlockSpec((k, tn), lambda i, j: (0, j)),
            pl.BlockSpec((tm, tn), lambda i, j: (i, j)),
        ],
        out_specs=pl.BlockSpec((tm, tn), lambda i, j: (i, j)),
        out_shape=jax.ShapeDtypeStruct((m, n), F32),
        compiler_params=_cparams("parallel", "arbitrary"),
        name="matmul_res",
    )(y, w, res)


def _rglru_kernel(x_ref, g_ref, cw_ref, cb_ref, wr_ref, br_ref, wi_ref, bi_ref, lam_ref, y_ref, a_s, b_s):
    seq, c = a_s.shape
    x = x_ref[0]
    row = lax.broadcasted_iota(jnp.int32, (seq, c), 0)
    xc = jnp.broadcast_to(cb_ref[...], (seq, c))
    for k in range(CONV_WIDTH):
        shift = CONV_WIDTH - 1 - k
        xs = x if shift == 0 else jnp.where(row >= shift, pltpu.roll(x, shift, 0), 0.0)
        xc = xc + xs * cw_ref[k:k + 1, :]

    lam = lam_ref[...]
    softplus_neg_lam = jnp.maximum(-lam, 0.0) + jnp.log1p(jnp.exp(-jnp.abs(lam)))
    for gi in range(c // RNN_BLOCK):
        sl = slice(gi * RNN_BLOCK, (gi + 1) * RNN_BLOCK)
        xcg = xc[:, sl]
        xb = xcg.astype(BF16)
        r = _sigmoid(_bdot(xb, wr_ref[gi].astype(BF16)) + br_ref[:, sl])
        ig = _sigmoid(_bdot(xb, wi_ref[gi].astype(BF16)) + bi_ref[:, sl])
        log_a = (-RGLRU_C) * r * softplus_neg_lam[:, sl]
        a = jnp.exp(log_a)
        a_s[:, sl] = a
        b_s[:, sl] = jnp.sqrt(jnp.tanh(-log_a) * (1.0 + a * a)) * (ig * xcg)

    sub = lax.broadcasted_iota(jnp.int32, (8, c), 0)
    tiles_per_trip = 6
    rows_per_trip = 8 * tiles_per_trip
    assert seq % rows_per_trip == 0

    def trip(i, h_prev):
        r0 = pl.multiple_of(i * rows_per_trip, 8)
        scanned = []
        for u in range(tiles_per_trip):
            a = a_s[pl.ds(r0 + 8 * u, 8), :]
            b = b_s[pl.ds(r0 + 8 * u, 8), :]
            for s in (1, 2, 4):
                keep = sub >= s
                a_sh = jnp.where(keep, pltpu.roll(a, s, 0), 1.0)
                b_sh = jnp.where(keep, pltpu.roll(b, s, 0), 0.0)
                b = a * b_sh + b
                a = a * a_sh
            scanned.append((a, b))
        for u, (a, b) in enumerate(scanned):
            h = a * h_prev + b
            b_s[pl.ds(r0 + 8 * u, 8), :] = h
            h_prev = jnp.broadcast_to(h[7:8, :], (8, c))
        return h_prev

    lax.fori_loop(0, seq // rows_per_trip, trip, jnp.zeros((8, c), F32))
    y_ref[0] = (b_s[...] * _gelu_tanh(g_ref[0])).astype(BF16)


def _rglru(xg, conv_w, conv_b, w_rg, b_rg, w_ig, b_ig, lam, *, c):
    bsz, seq, two_d = xg.shape
    d = two_d // 2
    nc = d // c
    gpb = c // RNN_BLOCK
    vec = lambda v: v.reshape(1, d)
    vspec = pl.BlockSpec((1, c), lambda b, j: (0, j))
    return pl.pallas_call(
        _rglru_kernel,
        grid=(bsz, nc),
        in_specs=[
            pl.BlockSpec((1, seq, c), lambda b, j: (b, 0, j)),
            pl.BlockSpec((1, seq, c), lambda b, j: (b, 0, nc + j)),
            pl.BlockSpec((CONV_WIDTH, c), lambda b, j: (0, j)),
            vspec,
            pl.BlockSpec((gpb, RNN_BLOCK, RNN_BLOCK), lambda b, j: (j, 0, 0)),
            vspec,
            pl.BlockSpec((gpb, RNN_BLOCK, RNN_BLOCK), lambda b, j: (j, 0, 0)),
            vspec,
            vspec,
        ],
        out_specs=pl.BlockSpec((1, seq, c), lambda b, j: (b, 0, j)),
        out_shape=jax.ShapeDtypeStruct((bsz, seq, d), BF16),
        scratch_shapes=[pltpu.VMEM((seq, c), F32), pltpu.VMEM((seq, c), F32)],
        compiler_params=_cparams("parallel", "parallel"),
        name="rglru",
    )(xg, xg, conv_w, vec(conv_b), w_rg, vec(b_rg), w_ig, vec(b_ig), vec(lam))


def _ffn_kernel(x_ref, g_ref, wg_ref, wu_ref, wd_ref, wgt_ref, wut_ref, wdt_ref, o_ref, xn_ref):
    j = pl.program_id(1)

    def contrib(wg, wu, wd):
        xn = xn_ref[...]
        h = _silu(_bdot(xn, wg[...].astype(BF16))) * _bdot(xn, wu[...].astype(BF16))
        return _bdot(h.astype(BF16), wd[...].astype(BF16))

    @pl.when(j == 0)
    def _():
        x = x_ref[...]
        xn_ref[...] = _rms(x, g_ref[...]).astype(BF16)
        o_ref[...] = x + contrib(wgt_ref, wut_ref, wdt_ref)

    o_ref[...] += contrib(wg_ref, wu_ref, wd_ref)


def _ffn(x, g, w_gate, w_up, w_down, *, tm, tf):
    m, d = x.shape
    tail_blk = FF_MAIN // FF_TAIL
    return pl.pallas_call(
        _ffn_kernel,
        grid=(m // tm, FF_MAIN // tf),
        in_specs=[
            pl.BlockSpec((tm, d), lambda i, j: (i, 0), pipeline_mode=pl.Buffered(1)),
            pl.BlockSpec((1, d), lambda i, j: (0, 0)),
            pl.BlockSpec((d, tf), lambda i, j: (0, j)),
            pl.BlockSpec((d, tf), lambda i, j: (0, j)),
            pl.BlockSpec((tf, d), lambda i, j: (j, 0)),
            pl.BlockSpec((d, FF_TAIL), lambda i, j: (0, tail_blk)),
            pl.BlockSpec((d, FF_TAIL), lambda i, j: (0, tail_blk)),
            pl.BlockSpec((FF_TAIL, d), lambda i, j: (tail_blk, 0)),
        ],
        out_specs=pl.BlockSpec((tm, d), lambda i, j: (i, 0), pipeline_mode=pl.Buffered(1)),
        out_shape=jax.ShapeDtypeStruct((m, d), F32),
        scratch_shapes=[pltpu.VMEM((tm, d), BF16)],
        compiler_params=_cparams("parallel", "arbitrary"),
        name="dense_ffn",
    )(x, g.reshape(1, d), w_gate, w_up, w_down, w_gate, w_up, w_down)


def _attn_kernel(slopes_ref, q1_ref, q2_ref, k1_ref, k2_ref, v_ref, lq1_ref, lk1_ref, lq2_ref, lk2_ref,
                 sg_ref, o_ref, *, tq, lambda_init):
    slope = slopes_ref[pl.program_id(1)] * LOG2E
    seq = q1_ref.shape[1]
    q_refs = (q1_ref, q2_ref)
    k_refs = (k1_ref, k2_ref)
    nt = (((1,), (1,)), ((), ()))
    lam = (jnp.exp(jnp.sum(lq1_ref[...] * lk1_ref[...], axis=-1, keepdims=True))
           - jnp.exp(jnp.sum(lq2_ref[...] * lk2_ref[...], axis=-1, keepdims=True)) + lambda_init)
    lane = lax.broadcasted_iota(jnp.int32, (1, LANES), 1)
    meta_mask = lane < N_META
    col = lax.broadcasted_iota(jnp.int32, (1, tq), 1)
    diag_mask = col <= lax.broadcasted_iota(jnp.int32, (tq, tq), 0)
    diag_bias = slope * col.astype(F32)

    for qi in range(seq // tq):
        r0 = qi * tq
        pieces = [(0, LANES, slope * (lane - (N_META + r0)).astype(F32), meta_mask),
                  (N_META + r0, tq, diag_bias, diag_mask)]
        if qi > 0:
            off_col = lax.broadcasted_iota(jnp.int32, (1, r0), 1)
            pieces.append((N_META, r0, slope * (off_col - r0).astype(F32), None))

        exps, scales = [], []
        for s in range(2):
            q = q_refs[s][0, r0:r0 + tq, :]
            scores = []
            for k0, rows, bias, mask in pieces:
                sc = lax.dot_general(q, k_refs[s][0, k0:k0 + rows, :], nt, preferred_element_type=F32) + bias
                scores.append(sc if mask is None else jnp.where(mask, sc, -jnp.inf))
            m = functools.reduce(jnp.maximum, [jnp.max(sc, axis=-1, keepdims=True) for sc in scores])
            e = [jnp.exp2(sc - m) for sc in scores]
            l = functools.reduce(jnp.add, [jnp.sum(x, axis=-1, keepdims=True) for x in e])
            exps.append(e)
            scales.append(1.0 / l if s == 0 else lam / l)

        o = None
        for (k0, rows, _, _), e1, e2 in zip(pieces, exps[0], exps[1]):
            p = (e1 * scales[0] - e2 * scales[1]).astype(BF16)
            pv = _bdot(p, v_ref[0, k0:k0 + rows, :])
            o = pv if o is None else o + pv
        o_ref[0, r0:r0 + tq, :] = (_rms(o, sg_ref[...]) * (1.0 - lambda_init)).astype(o_ref.dtype)


def _diff_attention(q, kv, lq1, lk1, lq2, lk2, subln_g, *, tq, lambda_init):
    bsz, seq, _ = q.shape
    full = kv.shape[1]
    slopes = jnp.asarray([2.0 ** (-8.0 * (i + 1) / N_HEADS) for i in range(N_HEADS)], F32)
    vec = lambda v: v.reshape(1, -1)
    lspec = pl.BlockSpec((1, HEAD_DIM), lambda b, h, s: (0, 0))
    grid_spec = pltpu.PrefetchScalarGridSpec(
        num_scalar_prefetch=1,
        grid=(bsz, N_HEADS),
        in_specs=[
            pl.BlockSpec((1, seq, HEAD_DIM), lambda b, h, s: (b, 0, h)),
            pl.BlockSpec((1, seq, HEAD_DIM), lambda b, h, s: (b, 0, N_HEADS + h)),
            pl.BlockSpec((1, full, HEAD_DIM), lambda b, h, s: (b, 0, h)),
            pl.BlockSpec((1, full, HEAD_DIM), lambda b, h, s: (b, 0, N_HEADS + h)),
            pl.BlockSpec((1, full, V_HEAD_DIM), lambda b, h, s: (b, 0, N_HEADS + h)),
            lspec, lspec, lspec, lspec,
            pl.BlockSpec((1, V_HEAD_DIM), lambda b, h, s: (0, 0)),
        ],
        out_specs=pl.BlockSpec((1, seq, V_HEAD_DIM), lambda b, h, s: (b, 0, h)),
    )
    return pl.pallas_call(
        functools.partial(_attn_kernel, tq=tq, lambda_init=lambda_init),
        grid_spec=grid_spec,
        out_shape=jax.ShapeDtypeStruct((bsz, seq, N_HEADS * V_HEAD_DIM), BF16),
        compiler_params=_cparams("parallel", "parallel"),
        name="diff_attention",
    )(slopes, q, q, kv, kv, kv, vec(lq1), vec(lk1), vec(lq2), vec(lk2), vec(subln_g))


def _router_kernel(x_ref, g_ref, r_ref, idx_ref, w_ref):
    u = _rms(x_ref[...], g_ref[...])
    r = r_ref[...]
    u_hi = u.astype(BF16)
    u_lo = (u - u_hi.astype(F32)).astype(BF16)
    r_hi = r.astype(BF16)
    r_lo = (r - r_hi.astype(F32)).astype(BF16)
    logits = _bdot(u_hi, r_hi) + (_bdot(u_hi, r_lo) + _bdot(u_lo, r_hi))
    lane = lax.broadcasted_iota(jnp.int32, logits.shape, 1).astype(F32)
    lg = jnp.where(lane < N_EXPERTS, logits, -jnp.inf)
    m1 = jnp.max(lg, axis=-1, keepdims=True)
    i1 = jnp.min(jnp.where(lg == m1, lane, float(LANES)), axis=-1, keepdims=True)
    lg2 = jnp.where(lane == i1, -jnp.inf, lg)
    m2 = jnp.max(lg2, axis=-1, keepdims=True)
    i2 = jnp.min(jnp.where(lg2 == m2, lane, float(LANES)), axis=-1, keepdims=True)
    e2 = jnp.exp(m2 - m1)
    w1 = 1.0 / (1.0 + e2)
    w2 = e2 / (1.0 + e2)
    idx_ref[...] = jnp.where(lane == 0.0, i1, jnp.where(lane == 1.0, i2, 0.0)).astype(jnp.int32)
    w_ref[...] = jnp.where(lane == 0.0, w1, jnp.where(lane == 1.0, w2, 0.0))


def _router(x, g, router, *, tm):
    m, d = x.shape
    router_p = jnp.pad(router, ((0, 0), (0, LANES - N_EXPERTS)))
    out = pl.BlockSpec((tm, LANES), lambda i: (i, 0))
    return pl.pallas_call(
        _router_kernel,
        grid=(m // tm,),
        in_specs=[
            pl.BlockSpec((tm, d), lambda i: (i, 0)),
            pl.BlockSpec((1, d), lambda i: (0, 0)),
            pl.BlockSpec((d, LANES), lambda i: (0, 0)),
        ],
        out_specs=[out, out],
        out_shape=[jax.ShapeDtypeStruct((m, LANES), jnp.int32), jax.ShapeDtypeStruct((m, LANES), F32)],
        compiler_params=_cparams("parallel"),
        name="moe_router",
    )(x, g.reshape(1, d), router_p)


def _gather_norm_kernel(src_ref, used_ref, h_hbm, g_ref, xs_ref, buf, sem, *, sub):
    k = pl.program_id(0)

    def issue(kk, slot):
        base = kk * sub

        def body(r8, carry):
            for u in range(8):
                r = r8 * 8 + u
                tok = src_ref[base + r]
                pltpu.make_async_copy(h_hbm.at[pl.ds(tok, 1), :], buf.at[slot, pl.ds(r, 1), :],
                                      sem.at[slot]).start(priority=u % 2)
            return carry
        lax.fori_loop(0, sub // 8, body, 0)

    @pl.when(jnp.logical_and(k == 0, used_ref[0] == 1))
    def _():
        issue(0, 0)

    @pl.when(k + 1 < pl.num_programs(0))
    def _():
        @pl.when(used_ref[k + 1] == 1)
        def _():
            issue(k + 1, (k + 1) & 1)

    @pl.when(used_ref[k] == 1)
    def _():
        slot = k & 1
        pltpu.make_async_copy(h_hbm.at[pl.ds(0, sub), :], buf.at[slot], sem.at[slot]).wait()
        xs_ref[...] = _rms(buf[slot], g_ref[...]).astype(BF16)

    @pl.when(used_ref[k] == 0)
    def _():
        xs_ref[...] = jnp.zeros_like(xs_ref)


def _gather_norm(h, g, src_token, used, *, sub):
    _, d = h.shape
    grid_spec = pltpu.PrefetchScalarGridSpec(
        num_scalar_prefetch=2,
        grid=(used.shape[0],),
        in_specs=[pl.BlockSpec(memory_space=pl.ANY), pl.BlockSpec((1, d), lambda k, s, u: (0, 0))],
        out_specs=pl.BlockSpec((sub, d), lambda k, s, u: (k, 0)),
        scratch_shapes=[pltpu.VMEM((2, sub, d), F32), pltpu.SemaphoreType.DMA((2,))],
    )
    return pl.pallas_call(
        functools.partial(_gather_norm_kernel, sub=sub),
        grid_spec=grid_spec,
        out_shape=jax.ShapeDtypeStruct((src_token.shape[0], d), BF16),
        compiler_params=_cparams("arbitrary"),
        name="moe_gather",
    )(src_token, used, h, g.reshape(1, d))


def _combine_kernel(y0_ref, y1_ref, h_ref, w_ref, g_ref, out_ref):
    w = w_ref[...]
    y = h_ref[...] + w[:, 0:1] * y0_ref[...] + w[:, 1:2] * y1_ref[...]
    out_ref[...] = _rms(y, g_ref[...])


def _combine(y_assign, h, wts, g, *, tm):
    m, d = h.shape
    nb = m // tm
    return pl.pallas_call(
        _combine_kernel,
        grid=(nb,),
        in_specs=[
            pl.BlockSpec((tm, d), lambda i: (i, 0)),
            pl.BlockSpec((tm, d), lambda i: (nb + i, 0)),
            pl.BlockSpec((tm, d), lambda i: (i, 0)),
            pl.BlockSpec((tm, LANES), lambda i: (i, 0)),
            pl.BlockSpec((1, d), lambda i: (0, 0)),
        ],
        out_specs=pl.BlockSpec((tm, d), lambda i: (i, 0)),
        out_shape=jax.ShapeDtypeStruct((m, d), F32),
        compiler_params=_cparams("parallel"),
        name="moe_combine",
    )(y_assign, y_assign, h, wts, g.reshape(1, d))


def _moe_ffn_kernel(te_ref, nsub_ref, nt_ref, dst_ref, x_ref, wg_ref, wu_ref, wd_ref, wgt_ref, wut_ref, wdt_ref,
                    y_hbm, acc, wgb, wub, wdb, wgtb, wutb, wdtb, sem, *, sub, n_assign):
    t = pl.program_id(0)
    j = pl.program_id(1)
    tile = acc.shape[0]
    ns = nsub_ref[t]

    def scatter_rows(start, size):
        def body(r8, carry):
            for u in range(8):
                r = start + r8 * 8 + u
                pltpu.make_async_copy(acc.at[pl.ds(r, 1), :], y_hbm.at[pl.ds(dst_ref[t * tile + r], 1), :],
                                      sem.at[0]).start(priority=u % 2)
            return carry
        lax.fori_loop(0, size // 8, body, 0)

    def block_copy(dst_row):
        return pltpu.make_async_copy(acc.at[pl.ds(0, sub), :], y_hbm.at[pl.ds(dst_row, sub), :], sem.at[0])

    def accumulate(wg, wu, wd, scatter):
        def rows_at(start, size):
            rows = pl.ds(pl.multiple_of(start, sub), size)
            x = x_ref[rows, :]
            h = _silu(_bdot(x, wg[...])) * _bdot(x, wu[...])
            acc[rows, :] += _bdot(h.astype(BF16), wd[...])
            if scatter:
                scatter_rows(start, size)

        def quad(p, carry):
            rows_at(p * (4 * sub), 4 * sub)
            return carry

        lax.fori_loop(0, lax.shift_right_logical(ns, 2), quad, 0)

        @pl.when((ns & 2) == 2)
        def _():
            rows_at((ns & ~3) * sub, 2 * sub)

        @pl.when((ns & 1) == 1)
        def _():
            rows_at((ns - 1) * sub, sub)

    @pl.when(t < nt_ref[0])
    def _():
        @pl.when(j == 0)
        def _():
            acc[...] = jnp.zeros_like(acc)

            @pl.when(t == 0)
            def _():
                spare = block_copy(n_assign)
                spare.start()
                spare.wait()

            wgtb[...] = wgt_ref[...].astype(BF16)
            wutb[...] = wut_ref[...].astype(BF16)
            wdtb[...] = wdt_ref[...].astype(BF16)
            accumulate(wgtb, wutb, wdtb, False)

        wgb[...] = wg_ref[...].astype(BF16)
        wub[...] = wu_ref[...].astype(BF16)
        wdb[...] = wd_ref[...].astype(BF16)
        last = j == pl.num_programs(1) - 1

        @pl.when(jnp.logical_not(last))
        def _():
            accumulate(wgb, wub, wdb, False)

        @pl.when(last)
        def _():
            accumulate(wgb, wub, wdb, True)

            def wait_block(s, carry):
                block_copy(0).wait()
                return carry
            lax.fori_loop(0, ns, wait_block, 0)


def _moe_ffn(xs, w_gate, w_up, w_down, plan, dst_row, *, tile, sub, tf, n_assign):
    npad, d = xs.shape
    nj = FF_MAIN // tf
    tail_blk = FF_MAIN // FF_TAIL

    def live(t, nt):
        return jnp.minimum(t, nt[0] - 1)

    def jj(t, j, nt):
        return jnp.where(t < nt[0], j, nj - 1)

    resident = dict(pipeline_mode=pl.Buffered(1))
    grid_spec = pltpu.PrefetchScalarGridSpec(
        num_scalar_prefetch=4,
        grid=(npad // tile, nj),
        in_specs=[
            pl.BlockSpec((tile, d), lambda t, j, te, ns, nt, ds: (live(t, nt), 0)),
            pl.BlockSpec((None, d, tf), lambda t, j, te, ns, nt, ds: (te[t], 0, jj(t, j, nt))),
            pl.BlockSpec((None, d, tf), lambda t, j, te, ns, nt, ds: (te[t], 0, jj(t, j, nt))),
            pl.BlockSpec((None, tf, d), lambda t, j, te, ns, nt, ds: (te[t], jj(t, j, nt), 0)),
            pl.BlockSpec((None, d, FF_TAIL), lambda t, j, te, ns, nt, ds: (te[t], 0, tail_blk), **resident),
            pl.BlockSpec((None, d, FF_TAIL), lambda t, j, te, ns, nt, ds: (te[t], 0, tail_blk), **resident),
            pl.BlockSpec((None, FF_TAIL, d), lambda t, j, te, ns, nt, ds: (te[t], tail_blk, 0), **resident),
        ],
        out_specs=pl.BlockSpec(memory_space=pl.ANY),
        scratch_shapes=[pltpu.VMEM((tile, d), F32),
                        pltpu.VMEM((d, tf), BF16), pltpu.VMEM((d, tf), BF16), pltpu.VMEM((tf, d), BF16),
                        pltpu.VMEM((d, FF_TAIL), BF16), pltpu.VMEM((d, FF_TAIL), BF16),
                        pltpu.VMEM((FF_TAIL, d), BF16), pltpu.SemaphoreType.DMA((1,))],
    )
    return pl.pallas_call(
        functools.partial(_moe_ffn_kernel, sub=sub, n_assign=n_assign),
        grid_spec=grid_spec,
        out_shape=jax.ShapeDtypeStruct((n_assign + sub, d), F32),
        compiler_params=_cparams("arbitrary", "arbitrary"),
        name="moe_ffn",
    )(*plan, dst_row, xs, w_gate, w_up, w_down, w_gate, w_up, w_down)


def _moe_plan(expert_idx, *, tile, sub, max_tiles):
    n = expert_idx.shape[0]
    flat = expert_idx.reshape(-1)
    onehot = (flat[:, None] == jnp.arange(N_EXPERTS, dtype=jnp.int32)[None, :]).astype(jnp.int32)
    incl = jnp.cumsum(onehot, axis=0)
    counts = incl[-1]
    rank = jnp.sum((incl - onehot) * onehot, axis=1)
    tiles_e = (counts + tile - 1) // tile
    tile_end = jnp.cumsum(tiles_e)
    tile_start = tile_end - tiles_e
    dest = jnp.sum(onehot * (tile_start * tile)[None, :], axis=1) + rank
    n_tiles = tile_end[-1:]
    tile_ids = jnp.arange(max_tiles, dtype=jnp.int32)
    te = jnp.sum((tile_ids[:, None] >= tile_end[None, :]).astype(jnp.int32), axis=1)
    last_e = jnp.sum((n_tiles - 1 >= tile_end).astype(jnp.int32))
    te = jnp.minimum(te, last_e)
    rows_used = jnp.clip(counts[te] - (tile_ids - tile_start[te]) * tile, 0, tile)
    nsub = jnp.where(tile_ids < n_tiles, (rows_used + sub - 1) // sub, 0).astype(jnp.int32)
    n_assign = n * TOP_K
    rows = jnp.arange(max_tiles * tile, dtype=jnp.int32)
    assign = jnp.full((max_tiles * tile,), -1, jnp.int32).at[dest].set(jnp.arange(n_assign, dtype=jnp.int32))
    src_token = jnp.maximum(assign, 0) // TOP_K
    dst_row = jnp.where(assign >= 0, (assign % TOP_K) * n + assign // TOP_K, n_assign + rows % sub)
    used = (jnp.arange(tile // sub, dtype=jnp.int32)[None, :] < nsub[:, None]).reshape(-1).astype(jnp.int32)
    plan = (te.astype(jnp.int32), nsub, n_tiles.astype(jnp.int32))
    return src_token, dst_row, used, plan


def kernel(x, meta_tokens, a_norm, a_w_in, a_conv_w, a_conv_b, a_w_rgate, a_b_rgate, a_w_igate, a_b_igate,
           a_lambda, a_w_out, kv_norm, w_kv, b_norm, b_w_q, b_lambda_q1, b_lambda_k1, b_lambda_q2, b_lambda_k2,
           b_subln, b_w_out, ffn_norm, ffn_w_gate, ffn_w_up, ffn_w_down, moe_router, moe_w_gate, moe_w_up,
           moe_w_down, final_norm):
    bsz, seq, d = x.shape
    full = N_META + seq
    tm0 = 688
    tm0_big = 2 * tm0
    tm1_big = 1024
    h = jnp.concatenate([jnp.broadcast_to(meta_tokens.astype(x.dtype)[None], (bsz, N_META, d)), x], axis=1)
    h = h.reshape(bsz * full, d)

    xg = _norm_matmul(h, a_norm[0], a_w_in[0], tm=tm0_big, tn=512, out_dtype=F32)
    y = _rglru(xg.reshape(bsz, full, 2 * d), a_conv_w[0], a_conv_b[0], a_w_rgate[0], a_b_rgate[0],
               a_w_igate[0], a_b_igate[0], a_lambda[0], c=256)
    h = _matmul_res(y.reshape(bsz * full, d), a_w_out[0], h, tm=tm0_big, tn=512)
    h = _ffn(h, ffn_norm[0], ffn_w_gate[0], ffn_w_up[0], ffn_w_down[0], tm=tm0_big, tf=256)

    kv = _norm_matmul(h, kv_norm, w_kv, tm=tm0_big, tn=512, out_dtype=BF16).reshape(bsz, full, -1)
    h = h.reshape(bsz, full, d)[:, N_META:].reshape(bsz * seq, d)

    lambda_init = 0.8 - 0.6 * math.exp(-0.3 * 1)
    q = _norm_matmul(h, b_norm[0], b_w_q[0], tm=tm1_big, tn=512, out_dtype=BF16, scale=HEAD_DIM ** -0.5 * LOG2E)
    o = _diff_attention(q.reshape(bsz, seq, -1), kv, b_lambda_q1[0], b_lambda_k1[0], b_lambda_q2[0],
                        b_lambda_k2[0], b_subln[0], tq=256, lambda_init=lambda_init)
    h = _matmul_res(o.reshape(bsz * seq, -1), b_w_out[0], h, tm=tm1_big, tn=512)

    n_tok = bsz * seq
    max_tiles = n_tok * TOP_K // MOE_TILE + N_EXPERTS
    idx, wts = _router(h, ffn_norm[1], moe_router[0], tm=512)
    src_token, dst_row, used, plan = _moe_plan(idx[:, :TOP_K], tile=MOE_TILE, sub=MOE_SUB, max_tiles=max_tiles)
    xs = _gather_norm(h, ffn_norm[1], src_token, used, sub=MOE_SUB)
    y_assign = _moe_ffn(xs, moe_w_gate[0], moe_w_up[0], moe_w_down[0], plan, dst_row, tile=MOE_TILE, sub=MOE_SUB,
                        tf=256, n_assign=n_tok * TOP_K)
    out = _combine(y_assign, h, wts, final_norm, tm=256)
    return out.reshape(bsz, seq, d)
```

```python
import functools
import math

import jax
import jax.numpy as jnp
from jax import lax
from jax.experimental import pallas as pl
from jax.experimental.pallas import tpu as pltpu

F32 = jnp.float32
BF16 = jnp.bfloat16

D_MODEL = 2048
N_META = 16
CONV_WIDTH = 4
RGLRU_C = 8.0
RNN_BLOCK = 128
HEAD_DIM = 128
N_HEADS = 8
V_HEAD_DIM = 2 * HEAD_DIM
D_FF = 5504
N_EXPERTS = 8
TOP_K = 2
EPS = 1e-6

LOG2E = math.log2(math.e)
LANES = 128
FF_TAIL = LANES
FF_MAIN = D_FF - FF_TAIL
VMEM_LIMIT = 58 * 1024 * 1024
MOE_SUB = 256
MOE_TILE = 9 * MOE_SUB


def _cparams(*sem):
    return pltpu.CompilerParams(dimension_semantics=sem, vmem_limit_bytes=VMEM_LIMIT)


def _rms(x, g):
    return x * lax.rsqrt(jnp.mean(x * x, axis=-1, keepdims=True) + EPS) * g


def _sigmoid(x):
    return 0.5 * jnp.tanh(0.5 * x) + 0.5


def _silu(x):
    return x * _sigmoid(x)


def _gelu_tanh(x):
    c = math.sqrt(2.0 / math.pi)
    return x * (0.5 * jnp.tanh(x * (c + (c * 0.044715) * (x * x))) + 0.5)


def _bdot(a, b):
    return jnp.dot(a, b, preferred_element_type=F32)


def _norm_matmul_kernel(x_ref, g_ref, w_ref, o_ref, xn_ref, *, scale):
    @pl.when(pl.program_id(1) == 0)
    def _():
        xn_ref[...] = _rms(x_ref[...], g_ref[...]).astype(BF16)

    acc = _bdot(xn_ref[...], w_ref[...].astype(BF16))
    if scale != 1.0:
        acc = acc * scale
    o_ref[...] = acc.astype(o_ref.dtype)


def _norm_matmul(x, g, w, *, tm, tn, out_dtype, scale=1.0):
    m, k = x.shape
    n = w.shape[1]
    return pl.pallas_call(
        functools.partial(_norm_matmul_kernel, scale=scale),
        grid=(m // tm, n // tn),
        in_specs=[
            pl.BlockSpec((tm, k), lambda i, j: (i, 0)),
            pl.BlockSpec((1, k), lambda i, j: (0, 0)),
            pl.BlockSpec((k, tn), lambda i, j: (0, j)),
        ],
        out_specs=pl.BlockSpec((tm, tn), lambda i, j: (i, j)),
        out_shape=jax.ShapeDtypeStruct((m, n), out_dtype),
        scratch_shapes=[pltpu.VMEM((tm, k), BF16)],
        compiler_params=_cparams("parallel", "arbitrary"),
        name="norm_matmul",
    )(x, g.reshape(1, k), w)


def _matmul_res_kernel(y_ref, w_ref, r_ref, o_ref):
    o_ref[...] = r_ref[...] + _bdot(y_ref[...], w_ref[...].astype(BF16))


def _matmul_res(y, w, res, *, tm, tn):
    m, k = y.shape
    n = w.shape[1]
    return pl.pallas_call(
        _matmul_res_kernel,
        grid=(m // tm, n // tn),
        in_specs=[
            pl.BlockSpec((tm, k), lambda i, j: (i, 0)),
            pl.BlockSpec((k, tn), lambda i, j: (0, j)),
            pl.BlockSpec((tm, tn), lambda i, j: (i, j)),
        ],
        out_specs=pl.BlockSpec((tm, tn), lambda i, j: (i, j)),
        out_shape=jax.ShapeDtypeStruct((m, n), F32),
        compiler_params=_cparams("parallel", "arbitrary"),
        name="matmul_res",
    )(y, w, res)


def _rglru_kernel(x_ref, g_ref, cw_ref, cb_ref, wr_ref, br_ref, wi_ref, bi_ref, lam_ref, y_ref, a_s, b_s):
    seq, c = a_s.shape
    x = x_ref[0]
    row = lax.broadcasted_iota(jnp.int32, (seq, c), 0)
    xc = jnp.broadcast_to(cb_ref[...], (seq, c))
    for k in range(CONV_WIDTH):
        shift = CONV_WIDTH - 1 - k
        xs = x if shift == 0 else jnp.where(row >= shift, pltpu.roll(x, shift, 0), 0.0)
        xc = xc + xs * cw_ref[k:k + 1, :]

    lam = lam_ref[...]
    softplus_neg_lam = jnp.maximum(-lam, 0.0) + jnp.log1p(jnp.exp(-jnp.abs(lam)))
    for gi in range(c // RNN_BLOCK):
        sl = slice(gi * RNN_BLOCK, (gi + 1) * RNN_BLOCK)
        xcg = xc[:, sl]
        xb = xcg.astype(BF16)
        r = _sigmoid(_bdot(xb, wr_ref[gi].astype(BF16)) + br_ref[:, sl])
        ig = _sigmoid(_bdot(xb, wi_ref[gi].astype(BF16)) + bi_ref[:, sl])
        log_a = (-RGLRU_C) * r * softplus_neg_lam[:, sl]
        a = jnp.exp(log_a)
        a_s[:, sl] = a
        b_s[:, sl] = jnp.sqrt(jnp.tanh(-log_a) * (1.0 + a * a)) * (ig * xcg)

    sub = lax.broadcasted_iota(jnp.int32, (8, c), 0)
    tiles_per_trip = 6
    rows_per_trip = 8 * tiles_per_trip
    assert seq % rows_per_trip == 0

    def trip(i, h_prev):
        r0 = pl.multiple_of(i * rows_per_trip, 8)
        scanned = []
        for u in range(tiles_per_trip):
            a = a_s[pl.ds(r0 + 8 * u, 8), :]
            b = b_s[pl.ds(r0 + 8 * u, 8), :]
            for s in (1, 2, 4):
                keep = sub >= s
                a_sh = jnp.where(keep, pltpu.roll(a, s, 0), 1.0)
                b_sh = jnp.where(keep, pltpu.roll(b, s, 0), 0.0)
                b = a * b_sh + b
                a = a * a_sh
            scanned.append((a, b))
        for u, (a, b) in enumerate(scanned):
            h = a * h_prev + b
            b_s[pl.ds(r0 + 8 * u, 8), :] = h
            h_prev = jnp.broadcast_to(h[7:8, :], (8, c))
        return h_prev

    lax.fori_loop(0, seq // rows_per_trip, trip, jnp.zeros((8, c), F32))
    y_ref[0] = (b_s[...] * _gelu_tanh(g_ref[0])).astype(BF16)


def _rglru(xg, conv_w, conv_b, w_rg, b_rg, w_ig, b_ig, lam, *, c):
    bsz, seq, two_d = xg.shape
    d = two_d // 2
    nc = d // c
    gpb = c // RNN_BLOCK
    vec = lambda v: v.reshape(1, d)
    vspec = pl.BlockSpec((1, c), lambda b, j: (0, j))
    return pl.pallas_call(
        _rglru_kernel,
        grid=(bsz, nc),
        in_specs=[
            pl.BlockSpec((1, seq, c), lambda b, j: (b, 0, j)),
            pl.BlockSpec((1, seq, c), lambda b, j: (b, 0, nc + j)),
            pl.BlockSpec((CONV_WIDTH, c), lambda b, j: (0, j)),
            vspec,
            pl.BlockSpec((gpb, RNN_BLOCK, RNN_BLOCK), lambda b, j: (j, 0, 0)),
            vspec,
            pl.BlockSpec((gpb, RNN_BLOCK, RNN_BLOCK), lambda b, j: (j, 0, 0)),
            vspec,
            vspec,
        ],
        out_specs=pl.BlockSpec((1, seq, c), lambda b, j: (b, 0, j)),
        out_shape=jax.ShapeDtypeStruct((bsz, seq, d), BF16),
        scratch_shapes=[pltpu.VMEM((seq, c), F32), pltpu.VMEM((seq, c), F32)],
        compiler_params=_cparams("parallel", "parallel"),
        name="rglru",
    )(xg, xg, conv_w, vec(conv_b), w_rg, vec(b_rg), w_ig, vec(b_ig), vec(lam))


def _ffn_kernel(x_ref, g_ref, wg_ref, wu_ref, wd_ref, wgt_ref, wut_ref, wdt_ref, o_ref, xn_ref):
    j = pl.program_id(1)

    def contrib(wg, wu, wd):
        xn = xn_ref[...]
        h = _silu(_bdot(xn, wg[...].astype(BF16))) * _bdot(xn, wu[...].astype(BF16))
        return _bdot(h.astype(BF16), wd[...].astype(BF16))

    @pl.when(j == 0)
    def _():
        x = x_ref[...]
        xn_ref[...] = _rms(x, g_ref[...]).astype(BF16)
        o_ref[...] = x + contrib(wgt_ref, wut_ref, wdt_ref)

    o_ref[...] += contrib(wg_ref, wu_ref, wd_ref)


def _ffn(x, g, w_gate, w_up, w_down, *, tm, tf):
    m, d = x.shape
    tail_blk = FF_MAIN // FF_TAIL
    return pl.pallas_call(
        _ffn_kernel,
        grid=(m // tm, FF_MAIN // tf),
        in_specs=[
            pl.BlockSpec((tm, d), lambda i, j: (i, 0), pipeline_mode=pl.Buffered(1)),
            pl.BlockSpec((1, d), lambda i, j: (0, 0)),
            pl.BlockSpec((d, tf), lambda i, j: (0, j)),
            pl.BlockSpec((d, tf), lambda i, j: (0, j)),
            pl.BlockSpec((tf, d), lambda i, j: (j, 0)),
            pl.BlockSpec((d, FF_TAIL), lambda i, j: (0, tail_blk)),
            pl.BlockSpec((d, FF_TAIL), lambda i, j: (0, tail_blk)),
            pl.BlockSpec((FF_TAIL, d), lambda i, j: (tail_blk, 0)),
        ],
        out_specs=pl.BlockSpec((tm, d), lambda i, j: (i, 0), pipeline_mode=pl.Buffered(1)),
        out_shape=jax.ShapeDtypeStruct((m, d), F32),
        scratch_shapes=[pltpu.VMEM((tm, d), BF16)],
        compiler_params=_cparams("parallel", "arbitrary"),
        name="dense_ffn",
    )(x, g.reshape(1, d), w_gate, w_up, w_down, w_gate, w_up, w_down)


def _attn_kernel(slopes_ref, q1_ref, q2_ref, k1_ref, k2_ref, v_ref, lq1_ref, lk1_ref, lq2_ref, lk2_ref,
                 sg_ref, o_ref, *, tq, lambda_init):
    slope = slopes_ref[pl.program_id(1)] * LOG2E
    seq = q1_ref.shape[1]
    q_refs = (q1_ref, q2_ref)
    k_refs = (k1_ref, k2_ref)
    nt = (((1,), (1,)), ((), ()))
    lam = (jnp.exp(jnp.sum(lq1_ref[...] * lk1_ref[...], axis=-1, keepdims=True))
           - jnp.exp(jnp.sum(lq2_ref[...] * lk2_ref[...], axis=-1, keepdims=True)) + lambda_init)
    lane = lax.broadcasted_iota(jnp.int32, (1, LANES), 1)
    meta_mask = lane < N_META
    col = lax.broadcasted_iota(jnp.int32, (1, tq), 1)
    diag_mask = col <= lax.broadcasted_iota(jnp.int32, (tq, tq), 0)
    diag_bias = slope * col.astype(F32)

    for qi in range(seq // tq):
        r0 = qi * tq
        pieces = [(0, LANES, slope * (lane - (N_META + r0)).astype(F32), meta_mask),
                  (N_META + r0, tq, diag_bias, diag_mask)]
        if qi > 0:
            off_col = lax.broadcasted_iota(jnp.int32, (1, r0), 1)
            pieces.append((N_META, r0, slope * (off_col - r0).astype(F32), None))

        exps, scales = [], []
        for s in range(2):
            q = q_refs[s][0, r0:r0 + tq, :]
            scores = []
            for k0, rows, bias, mask in pieces:
                sc = lax.dot_general(q, k_refs[s][0, k0:k0 + rows, :], nt, preferred_element_type=F32) + bias
                scores.append(sc if mask is None else jnp.where(mask, sc, -jnp.inf))
            m = functools.reduce(jnp.maximum, [jnp.max(sc, axis=-1, keepdims=True) for sc in scores])
            e = [jnp.exp2(sc - m) for sc in scores]
            l = functools.reduce(jnp.add, [jnp.sum(x, axis=-1, keepdims=True) for x in e])
            exps.append(e)
            scales.append(1.0 / l if s == 0 else lam / l)

        o = None
        for (k0, rows, _, _), e1, e2 in zip(pieces, exps[0], exps[1]):
            p = (e1 * scales[0] - e2 * scales[1]).astype(BF16)
            pv = _bdot(p, v_ref[0, k0:k0 + rows, :])
            o = pv if o is None else o + pv
        o_ref[0, r0:r0 + tq, :] = (_rms(o, sg_ref[...]) * (1.0 - lambda_init)).astype(o_ref.dtype)


def _diff_attention(q, kv, lq1, lk1, lq2, lk2, subln_g, *, tq, lambda_init):
    bsz, seq, _ = q.shape
    full = kv.shape[1]
    slopes = jnp.asarray([2.0 ** (-8.0 * (i + 1) / N_HEADS) for i in range(N_HEADS)], F32)
    vec = lambda v: v.reshape(1, -1)
    lspec = pl.BlockSpec((1, HEAD_DIM), lambda b, h, s: (0, 0))
    grid_spec = pltpu.PrefetchScalarGridSpec(
        num_scalar_prefetch=1,
        grid=(bsz, N_HEADS),
        in_specs=[
            pl.BlockSpec((1, seq, HEAD_DIM), lambda b, h, s: (b, 0, h)),
            pl.BlockSpec((1, seq, HEAD_DIM), lambda b, h, s: (b, 0, N_HEADS + h)),
            pl.BlockSpec((1, full, HEAD_DIM), lambda b, h, s: (b, 0, h)),
            pl.BlockSpec((1, full, HEAD_DIM), lambda b, h, s: (b, 0, N_HEADS + h)),
            pl.BlockSpec((1, full, V_HEAD_DIM), lambda b, h, s: (b, 0, N_HEADS + h)),
            lspec, lspec, lspec, lspec,
            pl.BlockSpec((1, V_HEAD_DIM), lambda b, h, s: (0, 0)),
        ],
        out_specs=pl.BlockSpec((1, seq, V_HEAD_DIM), lambda b, h, s: (b, 0, h)),
    )
    return pl.pallas_call(
        functools.partial(_attn_kernel, tq=tq, lambda_init=lambda_init),
        grid_spec=grid_spec,
        out_shape=jax.ShapeDtypeStruct((bsz, seq, N_HEADS * V_HEAD_DIM), BF16),
        compiler_params=_cparams("parallel", "parallel"),
        name="diff_attention",
    )(slopes, q, q, kv, kv, kv, vec(lq1), vec(lk1), vec(lq2), vec(lk2), vec(subln_g))


def _router_kernel(x_ref, g_ref, r_ref, idx_ref, w_ref):
    u = _rms(x_ref[...], g_ref[...])
    r = r_ref[...]
    u_hi = u.astype(BF16)
    u_lo = (u - u_hi.astype(F32)).astype(BF16)
    r_hi = r.astype(BF16)
    r_lo = (r - r_hi.astype(F32)).astype(BF16)
    logits = _bdot(u_hi, r_hi) + (_bdot(u_hi, r_lo) + _bdot(u_lo, r_hi))
    lane = lax.broadcasted_iota(jnp.int32, logits.shape, 1).astype(F32)
    lg = jnp.where(lane < N_EXPERTS, logits, -jnp.inf)
    m1 = jnp.max(lg, axis=-1, keepdims=True)
    i1 = jnp.min(jnp.where(lg == m1, lane, float(LANES)), axis=-1, keepdims=True)
    lg2 = jnp.where(lane == i1, -jnp.inf, lg)
    m2 = jnp.max(lg2, axis=-1, keepdims=True)
    i2 = jnp.min(jnp.where(lg2 == m2, lane, float(LANES)), axis=-1, keepdims=True)
    e2 = jnp.exp(m2 - m1)
    w1 = 1.0 / (1.0 + e2)
    w2 = e2 / (1.0 + e2)
    idx_ref[...] = jnp.where(lane == 0.0, i1, jnp.where(lane == 1.0, i2, 0.0)).astype(jnp.int32)
    w_ref[...] = jnp.where(lane == 0.0, w1, jnp.where(lane == 1.0, w2, 0.0))


def _router(x, g, router, *, tm):
    m, d = x.shape
    router_p = jnp.pad(router, ((0, 0), (0, LANES - N_EXPERTS)))
    out = pl.BlockSpec((tm, LANES), lambda i: (i, 0))
    return pl.pallas_call(
        _router_kernel,
        grid=(m // tm,),
        in_specs=[
            pl.BlockSpec((tm, d), lambda i: (i, 0)),
            pl.BlockSpec((1, d), lambda i: (0, 0)),
            pl.BlockSpec((d, LANES), lambda i: (0, 0)),
        ],
        out_specs=[out, out],
        out_shape=[jax.ShapeDtypeStruct((m, LANES), jnp.int32), jax.ShapeDtypeStruct((m, LANES), F32)],
        compiler_params=_cparams("parallel"),
        name="moe_router",
    )(x, g.reshape(1, d), router_p)


def _gather_norm_kernel(src_ref, used_ref, h_hbm, g_ref, xs_ref, buf, sem, *, sub):
    k = pl.program_id(0)

    def issue(kk, slot):
        base = kk * sub

        def body(r8, carry):
            for u in range(8):
                r = r8 * 8 + u
                tok = src_ref[base + r]
                pltpu.make_async_copy(h_hbm.at[pl.ds(tok, 1), :], buf.at[slot, pl.ds(r, 1), :],
                                      sem.at[slot]).start(priority=u % 2)
            return carry
        lax.fori_loop(0, sub // 8, body, 0)

    @pl.when(jnp.logical_and(k == 0, used_ref[0] == 1))
    def _():
        issue(0, 0)

    @pl.when(k + 1 < pl.num_programs(0))
    def _():
        @pl.when(used_ref[k + 1] == 1)
        def _():
            issue(k + 1, (k + 1) & 1)

    @pl.when(used_ref[k] == 1)
    def _():
        slot = k & 1
        pltpu.make_async_copy(h_hbm.at[pl.ds(0, sub), :], buf.at[slot], sem.at[slot]).wait()
        xs_ref[...] = _rms(buf[slot], g_ref[...]).astype(BF16)

    @pl.when(used_ref[k] == 0)
    def _():
        xs_ref[...] = jnp.zeros_like(xs_ref)


def _gather_norm(h, g, src_token, used, *, sub):
    _, d = h.shape
    grid_spec = pltpu.PrefetchScalarGridSpec(
        num_scalar_prefetch=2,
        grid=(used.shape[0],),
        in_specs=[pl.BlockSpec(memory_space=pl.ANY), pl.BlockSpec((1, d), lambda k, s, u: (0, 0))],
        out_specs=pl.BlockSpec((sub, d), lambda k, s, u: (k, 0)),
        scratch_shapes=[pltpu.VMEM((2, sub, d), F32), pltpu.SemaphoreType.DMA((2,))],
    )
    return pl.pallas_call(
        functools.partial(_gather_norm_kernel, sub=sub),
        grid_spec=grid_spec,
        out_shape=jax.ShapeDtypeStruct((src_token.shape[0], d), BF16),
        compiler_params=_cparams("arbitrary"),
        name="moe_gather",
    )(src_token, used, h, g.reshape(1, d))


def _combine_kernel(y0_ref, y1_ref, h_ref, w_ref, g_ref, out_ref):
    w = w_ref[...]
    y = h_ref[...] + w[:, 0:1] * y0_ref[...] + w[:, 1:2] * y1_ref[...]
    out_ref[...] = _rms(y, g_ref[...])


def _combine(y_assign, h, wts, g, *, tm):
    m, d = h.shape
    nb = m // tm
    return pl.pallas_call(
        _combine_kernel,
        grid=(nb,),
        in_specs=[
            pl.BlockSpec((tm, d), lambda i: (i, 0)),
            pl.BlockSpec((tm, d), lambda i: (nb + i, 0)),
            pl.BlockSpec((tm, d), lambda i: (i, 0)),
            pl.BlockSpec((tm, LANES), lambda i: (i, 0)),
            pl.BlockSpec((1, d), lambda i: (0, 0)),
        ],
        out_specs=pl.BlockSpec((tm, d), lambda i: (i, 0)),
        out_shape=jax.ShapeDtypeStruct((m, d), F32),
        compiler_params=_cparams("parallel"),
        name="moe_combine",
    )(y_assign, y_assign, h, wts, g.reshape(1, d))


def _moe_ffn_kernel(te_ref, nsub_ref, nt_ref, dst_ref, x_ref, wg_ref, wu_ref, wd_ref, wgt_ref, wut_ref, wdt_ref,
                    y_hbm, acc, wgb, wub, wdb, wgtb, wutb, wdtb, sem, *, sub, n_assign):
    t = pl.program_id(0)
    j = pl.program_id(1)
    tile = acc.shape[0]
    ns = nsub_ref[t]

    def scatter_rows(start, size):
        def body(r8, carry):
            for u in range(8):
                r = start + r8 * 8 + u
                pltpu.make_async_copy(acc.at[pl.ds(r, 1), :], y_hbm.at[pl.ds(dst_ref[t * tile + r], 1), :],
                                      sem.at[0]).start(priority=u % 2)
            return carry
        lax.fori_loop(0, size // 8, body, 0)

    def block_copy(dst_row):
        return pltpu.make_async_copy(acc.at[pl.ds(0, sub), :], y_hbm.at[pl.ds(dst_row, sub), :], sem.at[0])

    def accumulate(wg, wu, wd, scatter):
        def rows_at(start, size):
            rows = pl.ds(pl.multiple_of(start, sub), size)
            x = x_ref[rows, :]
            h = _silu(_bdot(x, wg[...])) * _bdot(x, wu[...])
            acc[rows, :] += _bdot(h.astype(BF16), wd[...])
            if scatter:
                scatter_rows(start, size)

        def quad(p, carry):
            rows_at(p * (4 * sub), 4 * sub)
            return carry

        lax.fori_loop(0, lax.shift_right_logical(ns, 2), quad, 0)

        @pl.when((ns & 2) == 2)
        def _():
            rows_at((ns & ~3) * sub, 2 * sub)

        @pl.when((ns & 1) == 1)
        def _():
            rows_at((ns - 1) * sub, sub)

    @pl.when(t < nt_ref[0])
    def _():
        @pl.when(j == 0)
        def _():
            acc[...] = jnp.zeros_like(acc)

            @pl.when(t == 0)
            def _():
                spare = block_copy(n_assign)
                spare.start()
                spare.wait()

            wgtb[...] = wgt_ref[...].astype(BF16)
            wutb[...] = wut_ref[...].astype(BF16)
            wdtb[...] = wdt_ref[...].astype(BF16)
            accumulate(wgtb, wutb, wdtb, False)

        wgb[...] = wg_ref[...].astype(BF16)
        wub[...] = wu_ref[...].astype(BF16)
        wdb[...] = wd_ref[...].astype(BF16)
        last = j == pl.num_programs(1) - 1

        @pl.when(jnp.logical_not(last))
        def _():
            accumulate(wgb, wub, wdb, False)

        @pl.when(last)
        def _():
            accumulate(wgb, wub, wdb, True)

            def wait_block(s, carry):
                block_copy(0).wait()
                return carry
            lax.fori_loop(0, ns, wait_block, 0)


def _moe_ffn(xs, w_gate, w_up, w_down, plan, dst_row, *, tile, sub, tf, n_assign):
    npad, d = xs.shape
    nj = FF_MAIN // tf
    tail_blk = FF_MAIN // FF_TAIL

    def live(t, nt):
        return jnp.minimum(t, nt[0] - 1)

    def jj(t, j, nt):
        return jnp.where(t < nt[0], j, nj - 1)

    resident = dict(pipeline_mode=pl.Buffered(1))
    grid_spec = pltpu.PrefetchScalarGridSpec(
        num_scalar_prefetch=4,
        grid=(npad // tile, nj),
        in_specs=[
            pl.BlockSpec((tile, d), lambda t, j, te, ns, nt, ds: (live(t, nt), 0)),
            pl.BlockSpec((None, d, tf), lambda t, j, te, ns, nt, ds: (te[t], 0, jj(t, j, nt))),
            pl.BlockSpec((None, d, tf), lambda t, j, te, ns, nt, ds: (te[t], 0, jj(t, j, nt))),
            pl.BlockSpec((None, tf, d), lambda t, j, te, ns, nt, ds: (te[t], jj(t, j, nt), 0)),
            pl.BlockSpec((None, d, FF_TAIL), lambda t, j, te, ns, nt, ds: (te[t], 0, tail_blk), **resident),
            pl.BlockSpec((None, d, FF_TAIL), lambda t, j, te, ns, nt, ds: (te[t], 0, tail_blk), **resident),
            pl.BlockSpec((None, FF_TAIL, d), lambda t, j, te, ns, nt, ds: (te[t], tail_blk, 0), **resident),
        ],
        out_specs=pl.BlockSpec(memory_space=pl.ANY),
        scratch_shapes=[pltpu.VMEM((tile, d), F32),
                        pltpu.VMEM((d, tf), BF16), pltpu.VMEM((d, tf), BF16), pltpu.VMEM((tf, d), BF16),
                        pltpu.VMEM((d, FF_TAIL), BF16), pltpu.VMEM((d, FF_TAIL), BF16),
                        pltpu.VMEM((FF_TAIL, d), BF16), pltpu.SemaphoreType.DMA((1,))],
    )
    return pl.pallas_call(
        functools.partial(_moe_ffn_kernel, sub=sub, n_assign=n_assign),
        grid_spec=grid_spec,
        out_shape=jax.ShapeDtypeStruct((n_assign + sub, d), F32),
        compiler_params=_cparams("arbitrary", "arbitrary"),
        name="moe_ffn",
    )(*plan, dst_row, xs, w_gate, w_up, w_down, w_gate, w_up, w_down)


def _moe_plan(expert_idx, *, tile, sub, max_tiles):
    n = expert_idx.shape[0]
    flat = expert_idx.reshape(-1)
    onehot = (flat[:, None] == jnp.arange(N_EXPERTS, dtype=jnp.int32)[None, :]).astype(jnp.int32)
    incl = jnp.cumsum(onehot, axis=0)
    counts = incl[-1]
    rank = jnp.sum((incl - onehot) * onehot, axis=1)
    tiles_e = (counts + tile - 1) // tile
    tile_end = jnp.cumsum(tiles_e)
    tile_start = tile_end - tiles_e
    dest = jnp.sum(onehot * (tile_start * tile)[None, :], axis=1) + rank
    n_tiles = tile_end[-1:]
    tile_ids = jnp.arange(max_tiles, dtype=jnp.int32)
    te = jnp.sum((tile_ids[:, None] >= tile_end[None, :]).astype(jnp.int32), axis=1)
    last_e = jnp.sum((n_tiles - 1 >= tile_end).astype(jnp.int32))
    te = jnp.minimum(te, last_e)
    rows_used = jnp.clip(counts[te] - (tile_ids - tile_start[te]) * tile, 0, tile)
    nsub = jnp.where(tile_ids < n_tiles, (rows_used + sub - 1) // sub, 0).astype(jnp.int32)
    n_assign = n * TOP_K
    rows = jnp.arange(max_tiles * tile, dtype=jnp.int32)
    assign = jnp.full((max_tiles * tile,), -1, jnp.int32).at[dest].set(jnp.arange(n_assign, dtype=jnp.int32))
    src_token = jnp.maximum(assign, 0) // TOP_K
    dst_row = jnp.where(assign >= 0, (assign % TOP_K) * n + assign // TOP_K, n_assign + rows % sub)
    used = (jnp.arange(tile // sub, dtype=jnp.int32)[None, :] < nsub[:, None]).reshape(-1).astype(jnp.int32)
    plan = (te.astype(jnp.int32), nsub, n_tiles.astype(jnp.int32))
    return src_token, dst_row, used, plan


def kernel(x, meta_tokens, a_norm, a_w_in, a_conv_w, a_conv_b, a_w_rgate, a_b_rgate, a_w_igate, a_b_igate,
           a_lambda, a_w_out, kv_norm, w_kv, b_norm, b_w_q, b_lambda_q1, b_lambda_k1, b_lambda_q2, b_lambda_k2,
           b_subln, b_w_out, ffn_norm, ffn_w_gate, ffn_w_up, ffn_w_down, moe_router, moe_w_gate, moe_w_up,
           moe_w_down, final_norm):
    bsz, seq, d = x.shape
    full = N_META + seq
    tm0 = 688
    tm0_big = 2 * tm0
    tm1_big = 2048
    h = jnp.concatenate([jnp.broadcast_to(meta_tokens.astype(x.dtype)[None], (bsz, N_META, d)), x], axis=1)
    h = h.reshape(bsz * full, d)

    xg = _norm_matmul(h, a_norm[0], a_w_in[0], tm=tm0_big, tn=512, out_dtype=F32)
    y = _rglru(xg.reshape(bsz, full, 2 * d), a_conv_w[0], a_conv_b[0], a_w_rgate[0], a_b_rgate[0],
               a_w_igate[0], a_b_igate[0], a_lambda[0], c=256)
    h = _matmul_res(y.reshape(bsz * full, d), a_w_out[0], h, tm=2 * tm0_big, tn=512)
    h = _ffn(h, ffn_norm[0], ffn_w_gate[0], ffn_w_up[0], ffn_w_down[0], tm=tm0_big, tf=256)

    kv = _norm_matmul(h, kv_norm, w_kv, tm=tm0_big, tn=512, out_dtype=BF16).reshape(bsz, full, -1)
    h = h.reshape(bsz, full, d)[:, N_META:].reshape(bsz * seq, d)

    lambda_init = 0.8 - 0.6 * math.exp(-0.3 * 1)
    q = _norm_matmul(h, b_norm[0], b_w_q[0], tm=tm1_big, tn=512, out_dtype=BF16, scale=HEAD_DIM ** -0.5 * LOG2E)
    o = _diff_attention(q.reshape(bsz, seq, -1), kv, b_lambda_q1[0], b_lambda_k1[0], b_lambda_q2[0],
                        b_lambda_k2[0], b_subln[0], tq=256, lambda_init=lambda_init)
    h = _matmul_res(o.reshape(bsz * seq, -1), b_w_out[0], h, tm=tm1_big, tn=512)

    n_tok = bsz * seq
    max_tiles = n_tok * TOP_K // MOE_TILE + N_EXPERTS
    idx, wts = _router(h, ffn_norm[1], moe_router[0], tm=512)
    src_token, dst_row, used, plan = _moe_plan(idx[:, :TOP_K], tile=MOE_TILE, sub=MOE_SUB, max_tiles=max_tiles)
    xs = _gather_norm(h, ffn_norm[1], src_token, used, sub=MOE_SUB)
    y_assign = _moe_ffn(xs, moe_w_gate[0], moe_w_up[0], moe_w_down[0], plan, dst_row, tile=MOE_TILE, sub=MOE_SUB,
                        tf=256, n_assign=n_tok * TOP_K)
    out = _combine(y_assign, h, wts, final_norm, tm=256)
    return out.reshape(bsz, seq, d)
```
